```python
import jax, jax.numpy as jnp
from jax import lax
import numpy as np

D_MODEL = 2048
BATCH = 16
SEQ = 2048
DEPTH = 1
DEC_BATCH = 8
DEC_SEQ = 32
PAST_LEN = 1024

CHUNK = 64
D_MIX = D_MODEL
D_A = D_MIX // 2
D_B = D_MIX - D_A
N_GROUPS_A = 8
HEAD_DIM_A = D_A // N_GROUPS_A
MLP_CHUNK = 128
CONV_WIDTH = 31
D_FF = 5632
N_MOD = 9
EPS = 1e-6

kernel_name = "hybrid_gmlp_conformer_stream_step"


def rms_norm(x, g):
    xf = x.astype(jnp.float32)
    y = xf * lax.rsqrt(jnp.mean(xf * xf, axis=-1, keepdims=True) + EPS)
    return (y * g.astype(jnp.float32)).astype(x.dtype)


def layer_norm(x, g, b):
    xf = x.astype(jnp.float32)
    mu = jnp.mean(xf, axis=-1, keepdims=True)
    xc = xf - mu
    y = xc * lax.rsqrt(jnp.mean(xc * xc, axis=-1, keepdims=True) + EPS)
    return (y * g.astype(jnp.float32) + b.astype(jnp.float32)).astype(x.dtype)


def swiglu_ffn(h, w_up, w_down):
    gu = h @ w_up
    g, u = jnp.split(gu, 2, axis=-1)
    return (jax.nn.silu(g) * u) @ w_down


def depthwise_causal_conv(xc, w_dw, b_dw):
    out = lax.conv_general_dilated(
        xc, w_dw[:, None, :].astype(xc.dtype), window_strides=(1,), padding='VALID',
        dimension_numbers=('NWC', 'WIO', 'NWC'), feature_group_count=xc.shape[-1])
    return out + b_dw


def _mixing(h, conv_hist, chunk_len, w_in, g_v, w_s_masked, b_s, w_dw, b_dw, g_cn, b_cn,
            g_out_a, g_out_b, w_out):
    z = h @ w_in
    u, v, ga, gb = jnp.split(z, [D_A, 2 * D_A, 2 * D_A + D_B], axis=-1)
    bsz, L, _ = v.shape
    v = rms_norm(v, g_v)
    vr = v.reshape(bsz, L // chunk_len, chunk_len, N_GROUPS_A, HEAD_DIM_A)
    ws = w_s_masked[:, :chunk_len, :chunk_len]
    bias = b_s[:, :chunk_len].T[None, None, :, :, None]
    sp = jnp.einsum('gij,bnjgd->bnigd', ws, vr) + bias
    y_a = u * sp.reshape(bsz, L, D_A)
    glu = ga * jax.nn.sigmoid(gb)
    if conv_hist is None:
        conv_hist = jnp.zeros((bsz, CONV_WIDTH - 1, D_B), glu.dtype)
    xc = jnp.concatenate([conv_hist.astype(glu.dtype), glu], axis=1)
    y_b = jax.nn.silu(layer_norm(depthwise_causal_conv(xc, w_dw, b_dw), g_cn, b_cn))
    new_conv = xc[:, -(CONV_WIDTH - 1):]
    y = jnp.concatenate([rms_norm(y_a, g_out_a), rms_norm(y_b, g_out_b)], axis=-1) @ w_out
    return y, new_conv, v


def _layer(x, c, conv_hist, chunk_len, w_ada, b_ada, g_ffn1, w_up1, w_down1, g_mix, w_in, g_v,
           w_s_masked, b_s, w_dw, b_dw, g_cn, b_cn, g_out_a, g_out_b, w_out, g_ffn2, w_up2, w_down2):
    mod = jax.nn.silu(c) @ w_ada + b_ada
    sh1, sc1, gt1, sh2, sc2, gt2, sh3, sc3, gt3 = [m[:, None, :] for m in jnp.split(mod, N_MOD, axis=-1)]
    h = rms_norm(x, g_ffn1) * (1 + sc1) + sh1
    x = x + 0.5 * gt1 * swiglu_ffn(h, w_up1, w_down1)
    h = rms_norm(x, g_mix) * (1 + sc2) + sh2
    y, new_conv, v = _mixing(h, conv_hist, chunk_len, w_in, g_v, w_s_masked, b_s, w_dw, b_dw,
                             g_cn, b_cn, g_out_a, g_out_b, w_out)
    x = x + gt2 * y
    h = rms_norm(x, g_ffn2) * (1 + sc3) + sh3
    x = x + 0.5 * gt3 * swiglu_ffn(h, w_up2, w_down2)
    return x, new_conv, v


def setup_inputs(seed: int = 0) -> dict:
    key = jax.random.key(seed)
    ks = jax.random.split(key, 32)
    n = lambda k, shape, s: jax.random.normal(k, shape, jnp.float32) * s
    L = DEPTH
    D = D_MODEL
    return {
        "x_prompt": n(ks[0], (BATCH, SEQ, D), 1.0),
        "x_sample": n(ks[1], (DEC_BATCH, DEC_SEQ, D), 1.0),
        "cache_conv": n(ks[2], (L, DEC_BATCH, CONV_WIDTH - 1, D_B), 0.5),
        "c_prompt": n(ks[3], (BATCH, D), 1.0),
        "c_sample": n(ks[4], (DEC_BATCH, D), 1.0),
        "w_ada": n(ks[5], (L, D, N_MOD * D), 0.5 * D ** -0.5),
        "b_ada": n(ks[6], (L, N_MOD * D), 0.02),
        "g_ffn1": 1.0 + n(ks[7], (L, D), 0.02),
        "w_up1": n(ks[8], (L, D, 2 * D_FF), D ** -0.5),
        "w_down1": n(ks[9], (L, D_FF, D), D_FF ** -0.5),
        "g_mix": 1.0 + n(ks[10], (L, D), 0.02),
        "w_in": n(ks[11], (L, D, 2 * D_A + 2 * D_B), D ** -0.5),
        "g_v": 1.0 + n(ks[12], (L, D_A), 0.02),
        "w_s": n(ks[13], (L, N_GROUPS_A, MLP_CHUNK, MLP_CHUNK), MLP_CHUNK ** -0.5),
        "b_s": n(ks[14], (L, N_GROUPS_A, MLP_CHUNK), 0.02),
        "w_dw": n(ks[15], (L, CONV_WIDTH, D_B), CONV_WIDTH ** -0.5),
        "b_dw": n(ks[16], (L, D_B), 0.02),
        "g_cn": 1.0 + n(ks[17], (L, D_B), 0.02),
        "b_cn": n(ks[18], (L, D_B), 0.02),
        "g_out_a": 1.0 + n(ks[19], (L, D_A), 0.02),
        "g_out_b": 1.0 + n(ks[20], (L, D_B), 0.02),
        "w_out": n(ks[21], (L, D_MIX, D), D_MIX ** -0.5),
        "g_ffn2": 1.0 + n(ks[22], (L, D), 0.02),
        "w_up2": n(ks[23], (L, D, 2 * D_FF), D ** -0.5),
        "w_down2": n(ks[24], (L, D_FF, D), D_FF ** -0.5),
        "g_final": 1.0 + n(ks[25], (D,), 0.02),
    }


def reference(x_prompt, x_sample, cache_conv, c_prompt, c_sample, w_ada, b_ada, g_ffn1, w_up1,
              w_down1, g_mix, w_in, g_v, w_s, b_s, w_dw, b_dw, g_cn, b_cn, g_out_a, g_out_b, w_out,
              g_ffn2, w_up2, w_down2, g_final):
    tri = jnp.tril(jnp.ones((MLP_CHUNK, MLP_CHUNK), dtype=w_s.dtype))
    sample_len = x_sample.shape[1]
    xp, xs = x_prompt, x_sample
    conv_p, conv_s, v_s = [], [], []
    for l in range(DEPTH):
        p = (w_ada[l], b_ada[l], g_ffn1[l], w_up1[l], w_down1[l], g_mix[l], w_in[l], g_v[l],
             w_s[l] * tri, b_s[l], w_dw[l], b_dw[l], g_cn[l], b_cn[l], g_out_a[l], g_out_b[l],
             w_out[l], g_ffn2[l], w_up2[l], w_down2[l])
        xp, cp, _ = _layer(xp, c_prompt, None, MLP_CHUNK, *p)
        xs, cs, vs = _layer(xs, c_sample, cache_conv[l], sample_len, *p)
        conv_p.append(cp)
        conv_s.append(cs)
        v_s.append(vs)
    y_prompt = rms_norm(xp, g_final)
    y_sample = rms_norm(xs, g_final)
    return (y_prompt, y_sample, jnp.stack(conv_p), jnp.stack(conv_s), jnp.stack(v_s))
```

```python
import functools

import jax
import jax.numpy as jnp
from jax import lax
from jax.experimental import pallas as pl
from jax.experimental.pallas import tpu as pltpu

EPS = 1e-6
N_MOD = 9
N_GROUPS_A = 8
HEAD_DIM_A = 128
CONV_WIDTH = 31
HIST = CONV_WIDTH - 1
HIST_PAD = 32
HIST_OFF = HIST_PAD - HIST

F32 = jnp.float32
BF16 = jnp.bfloat16

VMEM_LIMIT_BYTES = 56 * 1024 * 1024

MOD_TN = 1024
FFN_TM = 512
FFN_TF = 512
MIX_TM = 256
CONV_RT = 32
LANES = 128


def _sigmoid(x):
    return 1.0 / (1.0 + jnp.exp(-x))


def _silu(x):
    return x * _sigmoid(x)


def _rms(x, g):
    return (x * lax.rsqrt(jnp.mean(x * x, axis=-1, keepdims=True) + EPS)) * g


def _mod_kernel(c_ref, w_ref, b_ref, o_ref):
    s = _silu(c_ref[...]).astype(BF16)
    o_ref[...] = jnp.dot(s, w_ref[...].astype(BF16), preferred_element_type=F32) + b_ref[...]


def _modulation(c, w_ada, b_ada):
    rows, d = c.shape
    n = w_ada.shape[1]
    return pl.pallas_call(
        _mod_kernel,
        grid=(n // MOD_TN,),
        in_specs=[
            pl.BlockSpec((rows, d), lambda j: (0, 0)),
            pl.BlockSpec((d, MOD_TN), lambda j: (0, j)),
            pl.BlockSpec((1, MOD_TN), lambda j: (0, j)),
        ],
        out_specs=pl.BlockSpec((rows, MOD_TN), lambda j: (0, j)),
        out_shape=jax.ShapeDtypeStruct((rows, n), F32),
        compiler_params=pltpu.CompilerParams(
            dimension_semantics=("arbitrary",), vmem_limit_bytes=VMEM_LIMIT_BYTES),
        name="modulation",
    )(c, w_ada, b_ada.reshape(1, n))


def _ffn_kernel(x_ref, mod_ref, g_ref, wg_ref, wu_ref, wd_ref, gf_ref, o_ref, h_ref, acc_ref,
                *, mod_base, seg_rows, final_norm):
    j = pl.program_id(1)
    n_seg = mod_ref.shape[0]

    @pl.when(j == 0)
    def _():
        for s in range(n_seg):
            rows = pl.ds(s * seg_rows, seg_rows)
            sh = mod_ref[s, mod_base:mod_base + 1, :]
            sc = mod_ref[s, mod_base + 1:mod_base + 2, :]
            h = _rms(x_ref[rows, :], g_ref[...]) * (1.0 + sc) + sh
            h_ref[rows, :] = h.astype(BF16)
        acc_ref[...] = jnp.zeros_like(acc_ref)

    h = h_ref[...]
    g = jnp.dot(h, wg_ref[...], preferred_element_type=F32)
    u = jnp.dot(h, wu_ref[...], preferred_element_type=F32)
    a = (_silu(g) * u).astype(BF16)
    acc_ref[...] += jnp.dot(a, wd_ref[...], preferred_element_type=F32)

    @pl.when(j == pl.num_programs(1) - 1)
    def _():
        for s in range(n_seg):
            rows = pl.ds(s * seg_rows, seg_rows)
            gt = mod_ref[s, mod_base + 2:mod_base + 3, :]
            xo = x_ref[rows, :] + (0.5 * gt) * acc_ref[rows, :]
            if final_norm:
                xo = _rms(xo, gf_ref[...])
            o_ref[rows, :] = xo


def _ffn(x, mod, g, w_up, w_down, g_final, *, tm, seg_rows, tiles_per_mod, mod_base, final_norm):
    t, d = x.shape
    f = w_down.shape[0]
    nf = f // FFN_TF
    n_seg = tm // seg_rows
    kern = functools.partial(_ffn_kernel, mod_base=mod_base, seg_rows=seg_rows,
                             final_norm=final_norm)
    return pl.pallas_call(
        kern,
        grid=(t // tm, nf),
        in_specs=[
            pl.BlockSpec((tm, d), lambda i, j: (i, 0)),
            pl.BlockSpec((n_seg, N_MOD, d), lambda i, j: (i // tiles_per_mod, 0, 0)),
            pl.BlockSpec((1, d), lambda i, j: (0, 0)),
            pl.BlockSpec((d, FFN_TF), lambda i, j: (0, j)),
            pl.BlockSpec((d, FFN_TF), lambda i, j: (0, nf + j)),
            pl.BlockSpec((FFN_TF, d), lambda i, j: (j, 0)),
            pl.BlockSpec((1, d), lambda i, j: (0, 0)),
        ],
        out_specs=pl.BlockSpec((tm, d), lambda i, j: (i, 0)),
        out_shape=jax.ShapeDtypeStruct((t, d), F32),
        scratch_shapes=[pltpu.VMEM((tm, d), BF16), pltpu.VMEM((tm, d), F32)],
        compiler_params=pltpu.CompilerParams(
            dimension_semantics=("arbitrary", "arbitrary"), vmem_limit_bytes=VMEM_LIMIT_BYTES),
        name="swiglu_half_step",
    )(x, mod, g.reshape(1, d), w_up, w_up, w_down, g_final.reshape(1, d))


def _mix_kernel(x_ref, mod_ref, gmix_ref, win_ref, gv_ref, ws_ref, bias_ref, hist_ref, wdw_ref,
                bdw_ref, gcn_ref, bcn_ref, goa_ref, gob_ref, wout_ref,
                o_ref, conv_ref, *rest, chunk_len, emit_v):
    if emit_v:
        v_ref, h_ref, z_ref, ycat_ref, xc_ref = rest
    else:
        v_ref = None
        h_ref, z_ref, ycat_ref, xc_ref = rest
    tm = x_ref.shape[0]
    d_a = gv_ref.shape[1]
    d_b = gob_ref.shape[1]
    t = pl.program_id(1)

    @pl.when(t == 0)
    def _():
        xc_ref[0:HIST_PAD, :] = hist_ref[0]

    sh = mod_ref[0, 3:4, :]
    sc = mod_ref[0, 4:5, :]
    gt = mod_ref[0, 5:6, :]
    h_ref[...] = (_rms(x_ref[...], gmix_ref[...]) * (1.0 + sc) + sh).astype(BF16)

    z_ref[...] = jnp.dot(h_ref[...], win_ref[:, 0:2 * d_a], preferred_element_type=F32)
    row_i = lax.broadcasted_iota(jnp.int32, (chunk_len, chunk_len), 0)
    col_j = lax.broadcasted_iota(jnp.int32, (chunk_len, chunk_len), 1)
    tril = row_i >= col_j
    ws = [jnp.where(tril, ws_ref[g, 0:chunk_len, 0:chunk_len], 0.0).astype(BF16)
          for g in range(N_GROUPS_A)]
    for c in range(tm // chunk_len):
        rows = pl.ds(c * chunk_len, chunk_len)
        vn = _rms(z_ref[rows, d_a:2 * d_a], gv_ref[...])
        if emit_v:
            v_ref[0, rows, :] = vn
        vb = vn.astype(BF16)
        sp = jnp.concatenate(
            [jnp.dot(ws[g], vb[:, g * HEAD_DIM_A:(g + 1) * HEAD_DIM_A],
                     preferred_element_type=F32) for g in range(N_GROUPS_A)], axis=1)
        ya = z_ref[rows, 0:d_a] * (sp + bias_ref[0:chunk_len, :])
        ycat_ref[rows, 0:d_a] = _rms(ya, goa_ref[...]).astype(BF16)

    z_ref[...] = jnp.dot(h_ref[...], win_ref[:, 2 * d_a:2 * d_a + 2 * d_b],
                         preferred_element_type=F32)
    xc_ref[HIST_PAD:HIST_PAD + tm, :] = z_ref[:, 0:d_b] * _sigmoid(z_ref[:, d_b:2 * d_b])
    conv_ref[0] = xc_ref[tm + HIST_OFF:tm + HIST_PAD, :]

    win_rows = CONV_RT + HIST_PAD

    def conv_chunk(r, carry):
        r0 = pl.multiple_of(r * CONV_RT, CONV_RT)
        for ct in range(d_b // LANES):
            lanes = pl.ds(ct * LANES, LANES)
            window = xc_ref[pl.ds(r0, win_rows), lanes]
            acc = jnp.broadcast_to(bdw_ref[:, lanes], (CONV_RT, LANES))
            for k in range(CONV_WIDTH):
                off = k + HIST_OFF
                acc = acc + wdw_ref[k:k + 1, lanes] * window[off:off + CONV_RT, :]
            z_ref[pl.ds(r0, CONV_RT), lanes] = acc
        return carry

    lax.fori_loop(0, tm // CONV_RT, conv_chunk, 0)

    cv = z_ref[:, 0:d_b]
    mu = jnp.mean(cv, axis=-1, keepdims=True)
    cc = cv - mu
    ln = cc * lax.rsqrt(jnp.mean(cc * cc, axis=-1, keepdims=True) + EPS) * gcn_ref[...] + bcn_ref[...]
    ycat_ref[:, d_a:d_a + d_b] = _rms(_silu(ln), gob_ref[...]).astype(BF16)

    xc_ref[0:HIST_PAD, :] = xc_ref[tm:tm + HIST_PAD, :]

    y = jnp.dot(ycat_ref[...], wout_ref[...], preferred_element_type=F32)
    o_ref[...] = x_ref[...] + gt * y


def _mixing(x, mod, hist, g_mix, w_in, g_v, w_s, bias, w_dw, b_dw, g_cn, b_cn, g_out_a, g_out_b,
            w_out, *, batch, tm, chunk_len, emit_v):
    t, d = x.shape
    seq = t // batch
    nt = seq // tm
    d_a = g_v.shape[0]
    d_b = g_out_b.shape[0]
    const2 = lambda b, i: (0, 0)
    single = pl.Buffered(1)
    kern = functools.partial(_mix_kernel, chunk_len=chunk_len, emit_v=emit_v)
    out_shape = [jax.ShapeDtypeStruct((t, d), F32),
                 jax.ShapeDtypeStruct((batch, HIST, d_b), F32)]
    out_specs = [pl.BlockSpec((tm, d), lambda b, i: (b * nt + i, 0)),
                 pl.BlockSpec((1, HIST, d_b), lambda b, i: (b, 0, 0))]
    if emit_v:
        out_shape.append(jax.ShapeDtypeStruct((batch, seq, d_a), F32))
        out_specs.append(pl.BlockSpec((1, tm, d_a), lambda b, i: (b, i, 0)))
    return pl.pallas_call(
        kern,
        grid=(batch, nt),
        in_specs=[
            pl.BlockSpec((tm, d), lambda b, i: (b * nt + i, 0)),
            pl.BlockSpec((1, N_MOD, d), lambda b, i: (b, 0, 0)),
            pl.BlockSpec((1, d), const2),
            pl.BlockSpec(w_in.shape, const2, pipeline_mode=single),
            pl.BlockSpec((1, d_a), const2),
            pl.BlockSpec(w_s.shape, lambda b, i: (0, 0, 0)),
            pl.BlockSpec(bias.shape, const2),
            pl.BlockSpec((1, HIST_PAD, d_b), lambda b, i: (b, 0, 0)),
            pl.BlockSpec(w_dw.shape, const2),
            pl.BlockSpec((1, d_b), const2),
            pl.BlockSpec((1, d_b), const2),
            pl.BlockSpec((1, d_b), const2),
            pl.BlockSpec((1, d_a), const2),
            pl.BlockSpec((1, d_b), const2),
            pl.BlockSpec(w_out.shape, const2, pipeline_mode=single),
        ],
        out_specs=out_specs,
        out_shape=out_shape,
        scratch_shapes=[
            pltpu.VMEM((tm, d), BF16),
            pltpu.VMEM((tm, 2 * d_a), F32),
            pltpu.VMEM((tm, d_a + d_b), BF16),
            pltpu.VMEM((HIST_PAD + tm, d_b), F32),
        ],
        compiler_params=pltpu.CompilerParams(
            dimension_semantics=("arbitrary", "arbitrary"), vmem_limit_bytes=VMEM_LIMIT_BYTES),
        name="mixing_sublayer",
    )(x, mod, g_mix.reshape(1, d), w_in, g_v.reshape(1, d_a), w_s, bias, hist, w_dw,
      b_dw.reshape(1, d_b), g_cn.reshape(1, d_b), b_cn.reshape(1, d_b), g_out_a.reshape(1, d_a),
      g_out_b.reshape(1, d_b), w_out)


def kernel(x_prompt, x_sample, cache_conv, c_prompt, c_sample, w_ada, b_ada, g_ffn1, w_up1, w_down1, g_mix, w_in, g_v, w_s, b_s, w_dw, b_dw, g_cn, b_cn, g_out_a, g_out_b, w_out, g_ffn2, w_up2, w_down2, g_final):
    depth = w_ada.shape[0]
    bp, seq, d = x_prompt.shape
    bs, dec_seq, _ = x_sample.shape
    d_b = w_dw.shape[-1]
    mlp_chunk = w_s.shape[-1]

    xp = x_prompt.reshape(bp * seq, d)
    xs = x_sample.reshape(bs * dec_seq, d)
    c_all = jnp.concatenate([c_prompt, c_sample], axis=0)
    hist_p = jnp.zeros((bp, HIST_PAD, d_b), F32)

    conv_p, conv_s, v_s = [], [], []
    for l in range(depth):
        mod = _modulation(c_all, w_ada[l], b_ada[l]).reshape(bp + bs, N_MOD, d)
        mod_p, mod_s = mod[:bp], mod[bp:]
        wu1, wd1 = w_up1[l].astype(BF16), w_down1[l].astype(BF16)
        wu2, wd2 = w_up2[l].astype(BF16), w_down2[l].astype(BF16)
        wi, wo = w_in[l].astype(BF16), w_out[l].astype(BF16)
        bias = jnp.repeat(b_s[l].T, HEAD_DIM_A, axis=1)
        hist_s = jnp.pad(cache_conv[l], ((0, 0), (HIST_OFF, 0), (0, 0)))
        last = l == depth - 1
        mix_w = (g_mix[l], wi, g_v[l], w_s[l], bias, w_dw[l], b_dw[l], g_cn[l], b_cn[l],
                 g_out_a[l], g_out_b[l], wo)

        xp = _ffn(xp, mod_p, g_ffn1[l], wu1, wd1, g_final, tm=FFN_TM, seg_rows=FFN_TM, tiles_per_mod=seq // FFN_TM,
                  mod_base=0, final_norm=False)
        xs = _ffn(xs, mod_s, g_ffn1[l], wu1, wd1, g_final, tm=bs * dec_seq, seg_rows=dec_seq, tiles_per_mod=1,
                  mod_base=0, final_norm=False)

        xp, cp = _mixing(xp, mod_p, hist_p, *mix_w, batch=bp, tm=MIX_TM, chunk_len=mlp_chunk,
                         emit_v=False)
        xs, cs, vs = _mixing(xs, mod_s, hist_s, *mix_w, batch=bs, tm=dec_seq,
                             chunk_len=dec_seq, emit_v=True)

        xp = _ffn(xp, mod_p, g_ffn2[l], wu2, wd2, g_final, tm=FFN_TM, seg_rows=FFN_TM, tiles_per_mod=seq // FFN_TM,
                  mod_base=6, final_norm=last)
        xs = _ffn(xs, mod_s, g_ffn2[l], wu2, wd2, g_final, tm=bs * dec_seq, seg_rows=dec_seq, tiles_per_mod=1,
                  mod_base=6, final_norm=last)
        conv_p.append(cp)
        conv_s.append(cs)
        v_s.append(vs)

    if depth == 0:
        raise ValueError("depth must be at least 1")
    return (xp.reshape(bp, seq, d), xs.reshape(bs, dec_seq, d), jnp.stack(conv_p),
            jnp.stack(conv_s), jnp.stack(v_s))
```

```python
import functools

import jax
import jax.numpy as jnp
from jax import lax
from jax.experimental import pallas as pl
from jax.experimental.pallas import tpu as pltpu

EPS = 1e-6
N_MOD = 9
N_GROUPS_A = 8
HEAD_DIM_A = 128
CONV_WIDTH = 31
HIST = CONV_WIDTH - 1
HIST_PAD = 32
HIST_OFF = HIST_PAD - HIST

F32 = jnp.float32
BF16 = jnp.bfloat16

VMEM_LIMIT_BYTES = 56 * 1024 * 1024

MOD_TN = 1024
FFN_TM = 1024
FFN_TF = 512
FFN_DOWN_SPLIT = 2
MIX_TM = 256
CONV_RT = 32
LANES = 128
SUBLANES = 8


def _sigmoid(x):
    return 1.0 / (1.0 + jnp.exp(-x))


def _silu(x):
    return x * _sigmoid(x)


def _rms(x, g):
    return (x * lax.rsqrt(jnp.mean(x * x, axis=-1, keepdims=True) + EPS)) * g


def _mod_kernel(c_ref, w_ref, b_ref, o_ref):
    s = _silu(c_ref[...]).astype(BF16)
    o_ref[...] = jnp.dot(s, w_ref[...].astype(BF16), preferred_element_type=F32) + b_ref[...]


def _modulation(c, w_ada, b_ada):
    rows, d = c.shape
    n = w_ada.shape[1]
    return pl.pallas_call(
        _mod_kernel,
        grid=(n // MOD_TN,),
        in_specs=[
            pl.BlockSpec((rows, d), lambda j: (0, 0)),
            pl.BlockSpec((d, MOD_TN), lambda j: (0, j)),
            pl.BlockSpec((1, MOD_TN), lambda j: (0, j)),
        ],
        out_specs=pl.BlockSpec((rows, MOD_TN), lambda j: (0, j)),
        out_shape=jax.ShapeDtypeStruct((rows, n), F32),
        compiler_params=pltpu.CompilerParams(
            dimension_semantics=("arbitrary",), vmem_limit_bytes=VMEM_LIMIT_BYTES),
        name="modulation",
    )(c, w_ada, b_ada.reshape(1, n))


def _ffn_kernel(x_hbm, mod_ref, g_ref, wg_ref, wu_ref, wd_ref, gf_ref, o_ref, x_buf, h_ref, x_sem,
                *, mod_base, seg_rows, final_norm):
    i = pl.program_id(0)
    j = pl.program_id(1)
    tm, d = o_ref.shape
    n_seg = mod_ref.shape[0]

    def x_copy(tile):
        return pltpu.make_async_copy(x_hbm.at[pl.ds(tile * tm, tm), :], x_buf, x_sem)

    @pl.when(jnp.logical_and(i == 0, j == 0))
    def _():
        x_copy(0).start()

    @pl.when(j == 0)
    def _():
        x_copy(i).wait()
        for s in range(n_seg):
            rows = pl.ds(s * seg_rows, seg_rows)
            sh = mod_ref[s, mod_base:mod_base + 1, :]
            sc = mod_ref[s, mod_base + 1:mod_base + 2, :]
            x = x_buf[rows, :]
            h_ref[rows, :] = (_rms(x, g_ref[...]) * (1.0 + sc) + sh).astype(BF16)
            o_ref[rows, :] = x

    @pl.when(jnp.logical_and(j == 1, i + 1 < pl.num_programs(0)))
    def _():
        x_copy(i + 1).start()

    h = h_ref[...]
    g = jnp.dot(h, wg_ref[...], preferred_element_type=F32)
    u = jnp.dot(h, wu_ref[...], preferred_element_type=F32)
    a = (_silu(g) * u).astype(BF16)
    for n in range(FFN_DOWN_SPLIT):
        cols = pl.ds(n * (d // FFN_DOWN_SPLIT), d // FFN_DOWN_SPLIT)
        y = jnp.dot(a, wd_ref[:, cols], preferred_element_type=F32)
        for s in range(n_seg):
            rows = pl.ds(s * seg_rows, seg_rows)
            half_gate = 0.5 * mod_ref[s, mod_base + 2:mod_base + 3, cols]
            o_ref[rows, cols] += half_gate * y[s * seg_rows:(s + 1) * seg_rows, :]

    if final_norm:
        @pl.when(j == pl.num_programs(1) - 1)
        def _():
            o_ref[...] = _rms(o_ref[...], gf_ref[...])


def _ffn(x, mod, g, w_up, w_down, g_final, *, tm, seg_rows, tiles_per_mod, mod_base, final_norm):
    t, d = x.shape
    f = w_down.shape[0]
    nf = f // FFN_TF
    assert nf >= 2 and t % tm == 0 and tm % seg_rows == 0
    n_seg = tm // seg_rows
    kern = functools.partial(_ffn_kernel, mod_base=mod_base, seg_rows=seg_rows,
                             final_norm=final_norm)
    return pl.pallas_call(
        kern,
        grid=(t // tm, nf),
        in_specs=[
            pl.BlockSpec(memory_space=pl.ANY),
            pl.BlockSpec((n_seg, N_MOD, d), lambda i, j: (i // tiles_per_mod, 0, 0)),
            pl.BlockSpec((1, d), lambda i, j: (0, 0)),
            pl.BlockSpec((d, FFN_TF), lambda i, j: (0, j)),
            pl.BlockSpec((d, FFN_TF), lambda i, j: (0, nf + j)),
            pl.BlockSpec((FFN_TF, d), lambda i, j: (j, 0)),
            pl.BlockSpec((1, d), lambda i, j: (0, 0)),
        ],
        out_specs=pl.BlockSpec((tm, d), lambda i, j: (i, 0)),
        out_shape=jax.ShapeDtypeStruct((t, d), F32),
        scratch_shapes=[pltpu.VMEM((tm, d), F32), pltpu.VMEM((tm, d), BF16),
                        pltpu.SemaphoreType.DMA(())],
        compiler_params=pltpu.CompilerParams(
            dimension_semantics=("arbitrary", "arbitrary"), vmem_limit_bytes=VMEM_LIMIT_BYTES),
        name="swiglu_half_step",
    )(x, mod, g.reshape(1, d), w_up, w_up, w_down, g_final.reshape(1, d))


def _mix_kernel(x_ref, mod_ref, gmix_ref, win_ref, gv_ref, ws_ref, bias_ref, hist_ref, wdw_ref,
                bdw_ref, gcn_ref, bcn_ref, goa_ref, gob_ref, wout_ref,
                o_ref, conv_ref, *rest, chunk_len, emit_v):
    if emit_v:
        v_ref, h_ref, za_ref, zb_ref, ycat_ref, xc_ref = rest
    else:
        v_ref = None
        h_ref, za_ref, zb_ref, ycat_ref, xc_ref = rest
    tm = x_ref.shape[0]
    d_a = gv_ref.shape[1]
    d_b = gob_ref.shape[1]
    t = pl.program_id(1)

    @pl.when(t == 0)
    def _():
        xc_ref[0:HIST_PAD, :] = hist_ref[0]

    sh = mod_ref[0, 3:4, :]
    sc = mod_ref[0, 4:5, :]
    gt = mod_ref[0, 5:6, :]
    h_ref[...] = (_rms(x_ref[...], gmix_ref[...]) * (1.0 + sc) + sh).astype(BF16)

    zb_ref[...] = jnp.dot(h_ref[...], win_ref[:, 2 * d_a:2 * d_a + 2 * d_b],
                          preferred_element_type=F32)
    xc_ref[HIST_PAD:HIST_PAD + tm, :] = zb_ref[:, 0:d_b] * _sigmoid(zb_ref[:, d_b:2 * d_b])
    conv_ref[0] = xc_ref[tm + HIST_OFF:tm + HIST_PAD, :]

    za_ref[...] = jnp.dot(h_ref[...], win_ref[:, 0:2 * d_a], preferred_element_type=F32)

    win_rows = CONV_RT + HIST_PAD
    for ct in range(d_b // LANES):
        lanes = pl.ds(ct * LANES, LANES)
        for rc in range(tm // CONV_RT):
            r0 = rc * CONV_RT
            window = xc_ref[r0:r0 + win_rows, lanes]
            acc = jnp.broadcast_to(bdw_ref[:, lanes], (CONV_RT, LANES))
            for b in range(SUBLANES):
                part = None
                for off in range(HIST_OFF, HIST_PAD + 1):
                    if off % SUBLANES != b:
                        continue
                    k = off - HIST_OFF
                    term = wdw_ref[k:k + 1, lanes] * window[off:off + CONV_RT, :]
                    part = term if part is None else part + term
                acc = acc + part
            zb_ref[r0:r0 + CONV_RT, lanes] = acc

    row_i = lax.broadcasted_iota(jnp.int32, (chunk_len, chunk_len), 0)
    col_j = lax.broadcasted_iota(jnp.int32, (chunk_len, chunk_len), 1)
    tril = row_i >= col_j
    ws = [jnp.where(tril, ws_ref[g, 0:chunk_len, 0:chunk_len], 0.0).astype(BF16)
          for g in range(N_GROUPS_A)]
    for c in range(tm // chunk_len):
        rows = pl.ds(c * chunk_len, chunk_len)
        vn = _rms(za_ref[rows, d_a:2 * d_a], gv_ref[...])
        if emit_v:
            v_ref[0, rows, :] = vn
        vb = vn.astype(BF16)
        sp = jnp.concatenate(
            [jnp.dot(ws[g], vb[:, g * HEAD_DIM_A:(g + 1) * HEAD_DIM_A],
                     preferred_element_type=F32) for g in range(N_GROUPS_A)], axis=1)
        ya = za_ref[rows, 0:d_a] * (sp + bias_ref[0:chunk_len, :])
        ycat_ref[rows, 0:d_a] = _rms(ya, goa_ref[...]).astype(BF16)

    cv = zb_ref[:, 0:d_b]
    mu = jnp.mean(cv, axis=-1, keepdims=True)
    cc = cv - mu
    ln = cc * lax.rsqrt(jnp.mean(cc * cc, axis=-1, keepdims=True) + EPS) * gcn_ref[...] + bcn_ref[...]
    ycat_ref[:, d_a:d_a + d_b] = _rms(_silu(ln), gob_ref[...]).astype(BF16)

    xc_ref[0:HIST_PAD, :] = xc_ref[tm:tm + HIST_PAD, :]

    y = jnp.dot(ycat_ref[...], wout_ref[...], preferred_element_type=F32)
    o_ref[...] = x_ref[...] + gt * y


def _mixing(x, mod, hist, g_mix, w_in, g_v, w_s, bias, w_dw, b_dw, g_cn, b_cn, g_out_a, g_out_b,
            w_out, *, batch, tm, chunk_len, emit_v):
    t, d = x.shape
    seq = t // batch
    nt = seq // tm
    d_a = g_v.shape[0]
    d_b = g_out_b.shape[0]
    const2 = lambda b, i: (0, 0)
    single = pl.Buffered(1)
    kern = functools.partial(_mix_kernel, chunk_len=chunk_len, emit_v=emit_v)
    out_shape = [jax.ShapeDtypeStruct((t, d), F32),
                 jax.ShapeDtypeStruct((batch, HIST, d_b), F32)]
    out_specs = [pl.BlockSpec((tm, d), lambda b, i: (b * nt + i, 0)),
                 pl.BlockSpec((1, HIST, d_b), lambda b, i: (b, 0, 0))]
    if emit_v:
        out_shape.append(jax.ShapeDtypeStruct((batch, seq, d_a), F32))
        out_specs.append(pl.BlockSpec((1, tm, d_a), lambda b, i: (b, i, 0)))
    return pl.pallas_call(
        kern,
        grid=(batch, nt),
        in_specs=[
            pl.BlockSpec((tm, d), lambda b, i: (b * nt + i, 0)),
            pl.BlockSpec((1, N_MOD, d), lambda b, i: (b, 0, 0)),
            pl.BlockSpec((1, d), const2),
            pl.BlockSpec(w_in.shape, const2, pipeline_mode=single),
            pl.BlockSpec((1, d_a), const2),
            pl.BlockSpec(w_s.shape, lambda b, i: (0, 0, 0)),
            pl.BlockSpec(bias.shape, const2),
            pl.BlockSpec((1, HIST_PAD, d_b), lambda b, i: (b, 0, 0)),
            pl.BlockSpec(w_dw.shape, const2),
            pl.BlockSpec((1, d_b), const2),
            pl.BlockSpec((1, d_b), const2),
            pl.BlockSpec((1, d_b), const2),
            pl.BlockSpec((1, d_a), const2),
            pl.BlockSpec((1, d_b), const2),
            pl.BlockSpec(w_out.shape, const2, pipeline_mode=single),
        ],
        out_specs=out_specs,
        out_shape=out_shape,
        scratch_shapes=[
            pltpu.VMEM((tm, d), BF16),
            pltpu.VMEM((tm, 2 * d_a), F32),
            pltpu.VMEM((tm, 2 * d_b), F32),
            pltpu.VMEM((tm, d_a + d_b), BF16),
            pltpu.VMEM((HIST_PAD + tm, d_b), F32),
        ],
        compiler_params=pltpu.CompilerParams(
            dimension_semantics=("arbitrary", "arbitrary"), vmem_limit_bytes=VMEM_LIMIT_BYTES),
        name="mixing_sublayer",
    )(x, mod, g_mix.reshape(1, d), w_in, g_v.reshape(1, d_a), w_s, bias, hist, w_dw,
      b_dw.reshape(1, d_b), g_cn.reshape(1, d_b), b_cn.reshape(1, d_b), g_out_a.reshape(1, d_a),
      g_out_b.reshape(1, d_b), w_out)


def kernel(x_prompt, x_sample, cache_conv, c_prompt, c_sample, w_ada, b_ada, g_ffn1, w_up1, w_down1, g_mix, w_in, g_v, w_s, b_s, w_dw, b_dw, g_cn, b_cn, g_out_a, g_out_b, w_out, g_ffn2, w_up2, w_down2, g_final):
    depth = w_ada.shape[0]
    bp, seq, d = x_prompt.shape
    bs, dec_seq, _ = x_sample.shape
    d_b = w_dw.shape[-1]
    mlp_chunk = w_s.shape[-1]

    xp = x_prompt.reshape(bp * seq, d)
    xs = x_sample.reshape(bs * dec_seq, d)
    c_all = jnp.concatenate([c_prompt, c_sample], axis=0)
    hist_p = jnp.zeros((bp, HIST_PAD, d_b), F32)

    conv_p, conv_s, v_s = [], [], []
    for l in range(depth):
        mod = _modulation(c_all, w_ada[l], b_ada[l]).reshape(bp + bs, N_MOD, d)
        mod_p, mod_s = mod[:bp], mod[bp:]
        wu1, wd1 = w_up1[l].astype(BF16), w_down1[l].astype(BF16)
        wu2, wd2 = w_up2[l].astype(BF16), w_down2[l].astype(BF16)
        wi, wo = w_in[l].astype(BF16), w_out[l].astype(BF16)
        bias = jnp.repeat(b_s[l].T, HEAD_DIM_A, axis=1)
        hist_s = jnp.pad(cache_conv[l], ((0, 0), (HIST_OFF, 0), (0, 0)))
        last = l == depth - 1
        mix_w = (g_mix[l], wi, g_v[l], w_s[l], bias, w_dw[l], b_dw[l], g_cn[l], b_cn[l],
                 g_out_a[l], g_out_b[l], wo)

        xp = _ffn(xp, mod_p, g_ffn1[l], wu1, wd1, g_final, tm=FFN_TM, seg_rows=FFN_TM, tiles_per_mod=seq // FFN_TM,
                  mod_base=0, final_norm=False)
        xs = _ffn(xs, mod_s, g_ffn1[l], wu1, wd1, g_final, tm=bs * dec_seq, seg_rows=dec_seq, tiles_per_mod=1,
                  mod_base=0, final_norm=False)

        xp, cp = _mixing(xp, mod_p, hist_p, *mix_w, batch=bp, tm=MIX_TM, chunk_len=mlp_chunk,
                         emit_v=False)
        xs, cs, vs = _mixing(xs, mod_s, hist_s, *mix_w, batch=bs, tm=dec_seq,
                             chunk_len=dec_seq, emit_v=True)

        xp = _ffn(xp, mod_p, g_ffn2[l], wu2, wd2, g_final, tm=FFN_TM, seg_rows=FFN_TM, tiles_per_mod=seq // FFN_TM,
                  mod_base=6, final_norm=last)
        xs = _ffn(xs, mod_s, g_ffn2[l], wu2, wd2, g_final, tm=bs * dec_seq, seg_rows=dec_seq, tiles_per_mod=1,
                  mod_base=6, final_norm=last)
        conv_p.append(cp)
        conv_s.append(cs)
        v_s.append(vs)

    if depth == 0:
        raise ValueError("depth must be at least 1")
    return (xp.reshape(bp, seq, d), xs.reshape(bs, dec_seq, d), jnp.stack(conv_p),
            jnp.stack(conv_s), jnp.stack(v_s))
```

```python
import functools

import jax
import jax.numpy as jnp
from jax import lax
from jax.experimental import pallas as pl
from jax.experimental.pallas import tpu as pltpu

EPS = 1e-6
N_MOD = 9
N_GROUPS_A = 8
HEAD_DIM_A = 128
CONV_WIDTH = 31
HIST = CONV_WIDTH - 1
HIST_PAD = 32
HIST_OFF = HIST_PAD - HIST

F32 = jnp.float32
BF16 = jnp.bfloat16

VMEM_LIMIT_BYTES = 56 * 1024 * 1024

MOD_TN = 1024
FFN_TM = 1024
FFN_TF = 512
FFN_TF_SAMPLE_MAX = 1408
FFN_DOWN_SPLIT = 2
FFN_PROLOGUE_ROWS = 64
MIX_TM = 256
LANES = 128
XC_PITCH_PAD = 4
CV_PITCH_PAD = 8
CONV_TOKENS = 8
CONV_TAP_GROUP = 16


def _sigmoid(x):
    return 1.0 / (1.0 + jnp.exp(-x))


def _silu(x):
    return x * _sigmoid(x)


def _rms(x, g):
    return (x * lax.rsqrt(jnp.mean(x * x, axis=-1, keepdims=True) + EPS)) * g


def _mod_kernel(c_ref, w_ref, b_ref, o_ref):
    s = _silu(c_ref[...]).astype(BF16)
    o_ref[...] = jnp.dot(s, w_ref[...].astype(BF16), preferred_element_type=F32) + b_ref[...]


def _modulation(c, w_ada, b_ada):
    rows, d = c.shape
    n = w_ada.shape[1]
    return pl.pallas_call(
        _mod_kernel,
        grid=(n // MOD_TN,),
        in_specs=[
            pl.BlockSpec((rows, d), lambda j: (0, 0)),
            pl.BlockSpec((d, MOD_TN), lambda j: (0, j)),
            pl.BlockSpec((1, MOD_TN), lambda j: (0, j)),
        ],
        out_specs=pl.BlockSpec((rows, MOD_TN), lambda j: (0, j)),
        out_shape=jax.ShapeDtypeStruct((rows, n), F32),
        compiler_params=pltpu.CompilerParams(
            dimension_semantics=("arbitrary",), vmem_limit_bytes=VMEM_LIMIT_BYTES),
        name="modulation",
    )(c, w_ada, b_ada.reshape(1, n))


def _ffn_kernel(x_hbm, mod_ref, g_ref, wg_ref, wu_ref, wd_ref, gf_ref, o_ref, x_buf, h_ref, x_sem,
                *, mod_base, seg_rows, final_norm):
    i = pl.program_id(0)
    j = pl.program_id(1)
    tm, d = o_ref.shape
    n_seg = mod_ref.shape[0]

    def x_copy(tile):
        return pltpu.make_async_copy(x_hbm.at[pl.ds(tile * tm, tm), :], x_buf, x_sem)

    @pl.when(jnp.logical_and(i == 0, j == 0))
    def _():
        x_copy(0).start()

    @pl.when(j == 0)
    def _():
        x_copy(i).wait()
        chunk = min(seg_rows, FFN_PROLOGUE_ROWS)
        for s in range(n_seg):
            sh = mod_ref[s, mod_base:mod_base + 1, :]
            scale = g_ref[...] * (1.0 + mod_ref[s, mod_base + 1:mod_base + 2, :])
            for c in range(seg_rows // chunk):
                rows = pl.ds(s * seg_rows + c * chunk, chunk)
                x = x_buf[rows, :]
                inv = lax.rsqrt(jnp.mean(x * x, axis=-1, keepdims=True) + EPS)
                h_ref[rows, :] = ((x * inv) * scale + sh).astype(BF16)
                o_ref[rows, :] = x

    @pl.when(jnp.logical_and(j == 1, i + 1 < pl.num_programs(0)))
    def _():
        x_copy(i + 1).start()

    h = h_ref[...]
    g = jnp.dot(h, wg_ref[...], preferred_element_type=F32)
    u = jnp.dot(h, wu_ref[...], preferred_element_type=F32)
    a = (_silu(g) * u).astype(BF16)
    for n in range(FFN_DOWN_SPLIT):
        cols = pl.ds(n * (d // FFN_DOWN_SPLIT), d // FFN_DOWN_SPLIT)
        y = jnp.dot(a, wd_ref[:, cols], preferred_element_type=F32)
        for s in range(n_seg):
            rows = pl.ds(s * seg_rows, seg_rows)
            half_gate = 0.5 * mod_ref[s, mod_base + 2:mod_base + 3, cols]
            o_ref[rows, cols] += half_gate * y[s * seg_rows:(s + 1) * seg_rows, :]

    if final_norm:
        @pl.when(j == pl.num_programs(1) - 1)
        def _():
            o_ref[...] = _rms(o_ref[...], gf_ref[...])


def _hidden_tile(f, cap):
    for tf in range(cap - cap % LANES, 0, -LANES):
        if f % tf == 0 and f // tf >= 2:
            return tf
    raise ValueError(f"no hidden-width tile for {f}")


def _ffn(x, mod, g, w_up, w_down, g_final, *, tm, tf, seg_rows, tiles_per_mod, mod_base,
         final_norm):
    t, d = x.shape
    f = w_down.shape[0]
    nf = f // tf
    assert nf >= 2 and f % tf == 0 and t % tm == 0 and tm % seg_rows == 0
    n_seg = tm // seg_rows
    kern = functools.partial(_ffn_kernel, mod_base=mod_base, seg_rows=seg_rows,
                             final_norm=final_norm)
    return pl.pallas_call(
        kern,
        grid=(t // tm, nf),
        in_specs=[
            pl.BlockSpec(memory_space=pl.ANY),
            pl.BlockSpec((n_seg, N_MOD, d), lambda i, j: (i // tiles_per_mod, 0, 0)),
            pl.BlockSpec((1, d), lambda i, j: (0, 0)),
            pl.BlockSpec((d, tf), lambda i, j: (0, j)),
            pl.BlockSpec((d, tf), lambda i, j: (0, nf + j)),
            pl.BlockSpec((tf, d), lambda i, j: (j, 0)),
            pl.BlockSpec((1, d), lambda i, j: (0, 0)),
        ],
        out_specs=pl.BlockSpec((tm, d), lambda i, j: (i, 0)),
        out_shape=jax.ShapeDtypeStruct((t, d), F32),
        scratch_shapes=[pltpu.VMEM((tm, d), F32), pltpu.VMEM((tm, d), BF16),
                        pltpu.SemaphoreType.DMA(())],
        compiler_params=pltpu.CompilerParams(
            dimension_semantics=("arbitrary", "arbitrary"), vmem_limit_bytes=VMEM_LIMIT_BYTES),
        name="swiglu_half_step",
    )(x, mod, g.reshape(1, d), w_up, w_up, w_down, g_final.reshape(1, d))


def _mix_kernel(x_ref, mod_ref, gmix_ref, win_ref, gv_ref, ws_ref, bias_ref, hist_ref, wdw_ref,
                bdw_ref, gcn_ref, bcn_ref, goa_ref, gob_ref, wout_ref,
                o_ref, conv_ref, *rest, chunk_len, emit_v):
    if emit_v:
        v_ref, h_ref, za_ref, zb_ref, ycat_ref, xc_ref, cv_ref = rest
    else:
        v_ref = None
        h_ref, za_ref, zb_ref, ycat_ref, xc_ref, cv_ref = rest
    tm = x_ref.shape[0]
    d_a = gv_ref.shape[1]
    d_b = gob_ref.shape[1]
    n_slab = d_b // LANES
    pitch = HIST_PAD + tm + XC_PITCH_PAD
    t = pl.program_id(1)

    @pl.when(t == 0)
    def _():
        for j in range(n_slab):
            xc_ref[j * pitch:j * pitch + HIST_PAD, :] = hist_ref[0, :, j * LANES:(j + 1) * LANES]

    sh = mod_ref[0, 3:4, :]
    sc = mod_ref[0, 4:5, :]
    gt = mod_ref[0, 5:6, :]
    h_ref[...] = (_rms(x_ref[...], gmix_ref[...]) * (1.0 + sc) + sh).astype(BF16)

    zb_ref[...] = jnp.dot(h_ref[...], win_ref[:, 2 * d_a:2 * d_a + 2 * d_b],
                          preferred_element_type=F32)
    glu = zb_ref[:, 0:d_b] * _sigmoid(zb_ref[:, d_b:2 * d_b])
    for j in range(n_slab):
        xc_ref[j * pitch + HIST_PAD:j * pitch + HIST_PAD + tm, :] = glu[:, j * LANES:(j + 1) * LANES]
        conv_ref[0, :, j * LANES:(j + 1) * LANES] = (
            xc_ref[j * pitch + tm + HIST_OFF:j * pitch + tm + HIST_PAD, :])

    za_ref[...] = jnp.dot(h_ref[...], win_ref[:, 0:2 * d_a], preferred_element_type=F32)

    bias_b = bdw_ref[...]
    out_pitch = tm + CV_PITCH_PAD
    for t0 in range(0, tm, CONV_TOKENS):
        accs = [bias_b] * CONV_TOKENS
        for k0 in range(0, CONV_WIDTH, CONV_TAP_GROUP):
            k1 = min(k0 + CONV_TAP_GROUP, CONV_WIDTH)
            rows = {r: xc_ref[pl.ds(r + HIST_OFF, n_slab, stride=pitch), :]
                    for r in range(t0 + k0, t0 + CONV_TOKENS + k1 - 1)}
            for k in range(k0, k1):
                w_k = wdw_ref[k]
                accs = [accs[i] + w_k * rows[t0 + i + k] for i in range(CONV_TOKENS)]
        for i in range(CONV_TOKENS):
            cv_ref[pl.ds(t0 + i, n_slab, stride=out_pitch), :] = accs[i]

    for j in range(n_slab):
        xc_ref[j * pitch:j * pitch + HIST_PAD, :] = xc_ref[j * pitch + tm:j * pitch + tm + HIST_PAD, :]

    row_i = lax.broadcasted_iota(jnp.int32, (chunk_len, chunk_len), 0)
    col_j = lax.broadcasted_iota(jnp.int32, (chunk_len, chunk_len), 1)
    tril = row_i >= col_j
    ws = [jnp.where(tril, ws_ref[g, 0:chunk_len, 0:chunk_len], 0.0).astype(BF16)
          for g in range(N_GROUPS_A)]
    for c in range(tm // chunk_len):
        rows = pl.ds(c * chunk_len, chunk_len)
        vn = _rms(za_ref[rows, d_a:2 * d_a], gv_ref[...])
        if emit_v:
            v_ref[0, rows, :] = vn
        vb = vn.astype(BF16)
        sp = jnp.concatenate(
            [jnp.dot(ws[g], vb[:, g * HEAD_DIM_A:(g + 1) * HEAD_DIM_A],
                     preferred_element_type=F32) for g in range(N_GROUPS_A)], axis=1)
        ya = za_ref[rows, 0:d_a] * (sp + bias_ref[0:chunk_len, :])
        ycat_ref[rows, 0:d_a] = _rms(ya, goa_ref[...]).astype(BF16)

    cv = jnp.concatenate([cv_ref[j * out_pitch:j * out_pitch + tm, :] for j in range(n_slab)],
                         axis=1)
    mu = jnp.mean(cv, axis=-1, keepdims=True)
    cc = cv - mu
    ln = cc * lax.rsqrt(jnp.mean(cc * cc, axis=-1, keepdims=True) + EPS) * gcn_ref[...] + bcn_ref[...]
    ycat_ref[:, d_a:d_a + d_b] = _rms(_silu(ln), gob_ref[...]).astype(BF16)

    y = jnp.dot(ycat_ref[...], wout_ref[...], preferred_element_type=F32)
    o_ref[...] = x_ref[...] + gt * y


def _mixing(x, mod, hist, g_mix, w_in, g_v, w_s, bias, w_dw, b_dw, g_cn, b_cn, g_out_a, g_out_b,
            w_out, *, batch, tm, chunk_len, emit_v):
    t, d = x.shape
    seq = t // batch
    nt = seq // tm
    d_a = g_v.shape[0]
    d_b = g_out_b.shape[0]
    n_slab = d_b // LANES
    const2 = lambda b, i: (0, 0)
    single = pl.Buffered(1)
    kern = functools.partial(_mix_kernel, chunk_len=chunk_len, emit_v=emit_v)
    out_shape = [jax.ShapeDtypeStruct((t, d), F32),
                 jax.ShapeDtypeStruct((batch, HIST, d_b), F32)]
    out_specs = [pl.BlockSpec((tm, d), lambda b, i: (b * nt + i, 0)),
                 pl.BlockSpec((1, HIST, d_b), lambda b, i: (b, 0, 0))]
    if emit_v:
        out_shape.append(jax.ShapeDtypeStruct((batch, seq, d_a), F32))
        out_specs.append(pl.BlockSpec((1, tm, d_a), lambda b, i: (b, i, 0)))
    return pl.pallas_call(
        kern,
        grid=(batch, nt),
        in_specs=[
            pl.BlockSpec((tm, d), lambda b, i: (b * nt + i, 0)),
            pl.BlockSpec((1, N_MOD, d), lambda b, i: (b, 0, 0)),
            pl.BlockSpec((1, d), const2),
            pl.BlockSpec(w_in.shape, const2, pipeline_mode=single),
            pl.BlockSpec((1, d_a), const2),
            pl.BlockSpec(w_s.shape, lambda b, i: (0, 0, 0)),
            pl.BlockSpec(bias.shape, const2),
            pl.BlockSpec((1, HIST_PAD, d_b), lambda b, i: (b, 0, 0)),
            pl.BlockSpec((w_dw.shape[0], n_slab, LANES), lambda b, i: (0, 0, 0)),
            pl.BlockSpec((n_slab, LANES), const2),
            pl.BlockSpec((1, d_b), const2),
            pl.BlockSpec((1, d_b), const2),
            pl.BlockSpec((1, d_a), const2),
            pl.BlockSpec((1, d_b), const2),
            pl.BlockSpec(w_out.shape, const2, pipeline_mode=single),
        ],
        out_specs=out_specs,
        out_shape=out_shape,
        scratch_shapes=[
            pltpu.VMEM((tm, d), BF16),
            pltpu.VMEM((tm, 2 * d_a), F32),
            pltpu.VMEM((tm, 2 * d_b), F32),
            pltpu.VMEM((tm, d_a + d_b), BF16),
            pltpu.VMEM((n_slab * (HIST_PAD + tm + XC_PITCH_PAD), LANES), F32),
            pltpu.VMEM((n_slab * (tm + CV_PITCH_PAD), LANES), F32),
        ],
        compiler_params=pltpu.CompilerParams(
            dimension_semantics=("arbitrary", "arbitrary"), vmem_limit_bytes=VMEM_LIMIT_BYTES),
        name="mixing_sublayer",
    )(x, mod, g_mix.reshape(1, d), w_in, g_v.reshape(1, d_a), w_s, bias, hist,
      w_dw.reshape(w_dw.shape[0], n_slab, LANES), b_dw.reshape(n_slab, LANES),
      g_cn.reshape(1, d_b), b_cn.reshape(1, d_b), g_out_a.reshape(1, d_a),
      g_out_b.reshape(1, d_b), w_out)


def kernel(x_prompt, x_sample, cache_conv, c_prompt, c_sample, w_ada, b_ada, g_ffn1, w_up1, w_down1, g_mix, w_in, g_v, w_s, b_s, w_dw, b_dw, g_cn, b_cn, g_out_a, g_out_b, w_out, g_ffn2, w_up2, w_down2, g_final):
    depth = w_ada.shape[0]
    bp, seq, d = x_prompt.shape
    bs, dec_seq, _ = x_sample.shape
    d_b = w_dw.shape[-1]
    mlp_chunk = w_s.shape[-1]

    xp = x_prompt.reshape(bp * seq, d)
    xs = x_sample.reshape(bs * dec_seq, d)
    c_all = jnp.concatenate([c_prompt, c_sample], axis=0)
    hist_p = jnp.zeros((bp, HIST_PAD, d_b), F32)

    conv_p, conv_s, v_s = [], [], []
    for l in range(depth):
        mod = _modulation(c_all, w_ada[l], b_ada[l]).reshape(bp + bs, N_MOD, d)
        mod_p, mod_s = mod[:bp], mod[bp:]
        wu1, wd1 = w_up1[l].astype(BF16), w_down1[l].astype(BF16)
        wu2, wd2 = w_up2[l].astype(BF16), w_down2[l].astype(BF16)
        wi, wo = w_in[l].astype(BF16), w_out[l].astype(BF16)
        bias = jnp.repeat(b_s[l].T, HEAD_DIM_A, axis=1)
        hist_s = jnp.pad(cache_conv[l], ((0, 0), (HIST_OFF, 0), (0, 0)))
        last = l == depth - 1
        mix_w = (g_mix[l], wi, g_v[l], w_s[l], bias, w_dw[l], b_dw[l], g_cn[l], b_cn[l],
                 g_out_a[l], g_out_b[l], wo)
        prompt_tiles = dict(tm=FFN_TM, tf=FFN_TF, seg_rows=FFN_TM, tiles_per_mod=seq // FFN_TM)
        sample_tiles = dict(tm=bs * dec_seq, tf=_hidden_tile(wd1.shape[0], FFN_TF_SAMPLE_MAX),
                            seg_rows=dec_seq, tiles_per_mod=1)

        xp = _ffn(xp, mod_p, g_ffn1[l], wu1, wd1, g_final, mod_base=0, final_norm=False,
                  **prompt_tiles)
        xs = _ffn(xs, mod_s, g_ffn1[l], wu1, wd1, g_final, mod_base=0, final_norm=False,
                  **sample_tiles)

        xp, cp = _mixing(xp, mod_p, hist_p, *mix_w, batch=bp, tm=MIX_TM, chunk_len=mlp_chunk,
                         emit_v=False)
        xs, cs, vs = _mixing(xs, mod_s, hist_s, *mix_w, batch=bs, tm=dec_seq,
                             chunk_len=dec_seq, emit_v=True)

        xp = _ffn(xp, mod_p, g_ffn2[l], wu2, wd2, g_final, mod_base=6, final_norm=last,
                  **prompt_tiles)
        xs = _ffn(xs, mod_s, g_ffn2[l], wu2, wd2, g_final, mod_base=6, final_norm=last,
                  **sample_tiles)
        conv_p.append(cp)
        conv_s.append(cs)
        v_s.append(vs)

    if depth == 0:
        raise ValueError("depth must be at least 1")
    return (xp.reshape(bp, seq, d), xs.reshape(bs, dec_seq, d), jnp.stack(conv_p),
            jnp.stack(conv_s), jnp.stack(v_s))
```

```python
import functools

import jax
import jax.numpy as jnp
from jax import lax
from jax.experimental import pallas as pl
from jax.experimental.pallas import tpu as pltpu

EPS = 1e-6
N_MOD = 9
N_GROUPS_A = 8
HEAD_DIM_A = 128
CONV_WIDTH = 31
HIST = CONV_WIDTH - 1
HIST_PAD = 32
HIST_OFF = HIST_PAD - HIST

F32 = jnp.float32
BF16 = jnp.bfloat16

VMEM_LIMIT_BYTES = 56 * 1024 * 1024

MOD_TN = 1024
FFN_TM = 1024
FFN_TF = 512
FFN_TF_SAMPLE_MAX = 1408
FFN_DOWN_SPLIT = 2
FFN_PROLOGUE_ROWS = 64
FFN_OPEN_PARTS = 2
MIX_TM = 256
LANES = 128
XC_PITCH_PAD = 4
CV_PITCH_PAD = 8
CONV_TOKENS = 8
CONV_TAP_GROUP = 16


def _sigmoid(x):
    return 1.0 / (1.0 + jnp.exp(-x))


def _silu(x):
    return x * _sigmoid(x)


def _rms(x, g):
    return (x * lax.rsqrt(jnp.mean(x * x, axis=-1, keepdims=True) + EPS)) * g


def _mod_kernel(c_ref, w_ref, b_ref, o_ref):
    s = _silu(c_ref[...]).astype(BF16)
    o_ref[...] = jnp.dot(s, w_ref[...].astype(BF16), preferred_element_type=F32) + b_ref[...]


def _modulation(c, w_ada, b_ada):
    rows, d = c.shape
    n = w_ada.shape[1]
    return pl.pallas_call(
        _mod_kernel,
        grid=(n // MOD_TN,),
        in_specs=[
            pl.BlockSpec((rows, d), lambda j: (0, 0)),
            pl.BlockSpec((d, MOD_TN), lambda j: (0, j)),
            pl.BlockSpec((1, MOD_TN), lambda j: (0, j)),
        ],
        out_specs=pl.BlockSpec((rows, MOD_TN), lambda j: (0, j)),
        out_shape=jax.ShapeDtypeStruct((rows, n), F32),
        compiler_params=pltpu.CompilerParams(
            dimension_semantics=("arbitrary",), vmem_limit_bytes=VMEM_LIMIT_BYTES),
        name="modulation",
    )(c, w_ada, b_ada.reshape(1, n))


def _ffn_kernel(x_hbm, mod_ref, g_ref, wg_ref, wu_ref, wd_ref, gf_ref, o_ref, x_buf, h_ref, x_sem,
                *, mod_base, seg_rows, final_norm):
    i = pl.program_id(0)
    j = pl.program_id(1)
    tm, d = o_ref.shape
    n_seg = mod_ref.shape[0]

    def x_copy(tile):
        return pltpu.make_async_copy(x_hbm.at[pl.ds(tile * tm, tm), :], x_buf, x_sem)

    @pl.when(jnp.logical_and(i == 0, j == 0))
    def _():
        x_copy(0).start()

    def swiglu_rows(r0, n_rows, first):
        h = h_ref[pl.ds(r0, n_rows), :]
        g = jnp.dot(h, wg_ref[...], preferred_element_type=F32)
        u = jnp.dot(h, wu_ref[...], preferred_element_type=F32)
        a = (_silu(g) * u).astype(BF16)
        for n in range(FFN_DOWN_SPLIT):
            cols = pl.ds(n * (d // FFN_DOWN_SPLIT), d // FFN_DOWN_SPLIT)
            y = jnp.dot(a, wd_ref[:, cols], preferred_element_type=F32)
            start = r0
            while start < r0 + n_rows:
                s = start // seg_rows
                stop = min((s + 1) * seg_rows, r0 + n_rows)
                rows = pl.ds(start, stop - start)
                half_gate = 0.5 * mod_ref[s, mod_base + 2:mod_base + 3, cols]
                update = half_gate * y[start - r0:stop - r0, :]
                if first:
                    o_ref[rows, cols] = x_buf[rows, cols] + update
                else:
                    o_ref[rows, cols] += update
                start = stop

    def open_rows(s, r0, n_rows):
        sh = mod_ref[s, mod_base:mod_base + 1, :]
        scale = g_ref[...] * (1.0 + mod_ref[s, mod_base + 1:mod_base + 2, :])
        chunk = min(n_rows, FFN_PROLOGUE_ROWS)
        for c in range(n_rows // chunk):
            rows = pl.ds(r0 + c * chunk, chunk)
            x = x_buf[rows, :]
            inv = lax.rsqrt(jnp.mean(x * x, axis=-1, keepdims=True) + EPS)
            h_ref[rows, :] = ((x * inv) * scale + sh).astype(BF16)

    @pl.when(j == 0)
    def _():
        x_copy(i).wait()
        if n_seg == 1:
            part = tm // FFN_OPEN_PARTS
            for p in range(FFN_OPEN_PARTS):
                open_rows(0, p * part, part)
                swiglu_rows(p * part, part, first=True)
        else:
            for s in range(n_seg):
                open_rows(s, s * seg_rows, seg_rows)
            swiglu_rows(0, tm, first=True)

    @pl.when(jnp.logical_and(j == 1, i + 1 < pl.num_programs(0)))
    def _():
        x_copy(i + 1).start()

    @pl.when(j > 0)
    def _():
        swiglu_rows(0, tm, first=False)

    if final_norm:
        @pl.when(j == pl.num_programs(1) - 1)
        def _():
            o_ref[...] = _rms(o_ref[...], gf_ref[...])


def _hidden_tile(f, cap):
    for tf in range(cap - cap % LANES, 0, -LANES):
        if f % tf == 0 and f // tf >= 2:
            return tf
    raise ValueError(f"no hidden-width tile for {f}")


def _ffn(x, mod, g, w_up, w_down, g_final, *, tm, tf, seg_rows, tiles_per_mod, mod_base,
         final_norm):
    t, d = x.shape
    f = w_down.shape[0]
    nf = f // tf
    assert nf >= 2 and f % tf == 0 and t % tm == 0 and tm % seg_rows == 0
    n_seg = tm // seg_rows
    kern = functools.partial(_ffn_kernel, mod_base=mod_base, seg_rows=seg_rows,
                             final_norm=final_norm)
    return pl.pallas_call(
        kern,
        grid=(t // tm, nf),
        in_specs=[
            pl.BlockSpec(memory_space=pl.ANY),
            pl.BlockSpec((n_seg, N_MOD, d), lambda i, j: (i // tiles_per_mod, 0, 0)),
            pl.BlockSpec((1, d), lambda i, j: (0, 0)),
            pl.BlockSpec((d, tf), lambda i, j: (0, j)),
            pl.BlockSpec((d, tf), lambda i, j: (0, nf + j)),
            pl.BlockSpec((tf, d), lambda i, j: (j, 0)),
            pl.BlockSpec((1, d), lambda i, j: (0, 0)),
        ],
        out_specs=pl.BlockSpec((tm, d), lambda i, j: (i, 0)),
        out_shape=jax.ShapeDtypeStruct((t, d), F32),
        scratch_shapes=[pltpu.VMEM((tm, d), F32), pltpu.VMEM((tm, d), BF16),
                        pltpu.SemaphoreType.DMA(())],
        compiler_params=pltpu.CompilerParams(
            dimension_semantics=("arbitrary", "arbitrary"), vmem_limit_bytes=VMEM_LIMIT_BYTES),
        name="swiglu_half_step",
    )(x, mod, g.reshape(1, d), w_up, w_up, w_down, g_final.reshape(1, d))


def _mix_kernel(x_ref, mod_ref, gmix_ref, win_ref, gv_ref, ws_ref, bias_ref, hist_ref, wdw_ref,
                bdw_ref, gcn_ref, bcn_ref, goa_ref, gob_ref, wout_ref,
                o_ref, conv_ref, *rest, chunk_len, emit_v):
    if emit_v:
        v_ref, h_ref, za_ref, zb_ref, ycat_ref, xc_ref, cv_ref = rest
    else:
        v_ref = None
        h_ref, za_ref, zb_ref, ycat_ref, xc_ref, cv_ref = rest
    tm = x_ref.shape[0]
    d_a = gv_ref.shape[1]
    d_b = gob_ref.shape[1]
    n_slab = d_b // LANES
    pitch = HIST_PAD + tm + XC_PITCH_PAD
    t = pl.program_id(1)

    @pl.when(t == 0)
    def _():
        for j in range(n_slab):
            xc_ref[j * pitch:j * pitch + HIST_PAD, :] = hist_ref[0, :, j * LANES:(j + 1) * LANES]

    sh = mod_ref[0, 3:4, :]
    sc = mod_ref[0, 4:5, :]
    gt = mod_ref[0, 5:6, :]
    h_ref[...] = (_rms(x_ref[...], gmix_ref[...]) * (1.0 + sc) + sh).astype(BF16)

    zb_ref[...] = jnp.dot(h_ref[...], win_ref[:, 2 * d_a:2 * d_a + 2 * d_b],
                          preferred_element_type=F32)
    glu = zb_ref[:, 0:d_b] * _sigmoid(zb_ref[:, d_b:2 * d_b])
    for j in range(n_slab):
        xc_ref[j * pitch + HIST_PAD:j * pitch + HIST_PAD + tm, :] = glu[:, j * LANES:(j + 1) * LANES]
        conv_ref[0, :, j * LANES:(j + 1) * LANES] = (
            xc_ref[j * pitch + tm + HIST_OFF:j * pitch + tm + HIST_PAD, :])

    za_ref[...] = jnp.dot(h_ref[...], win_ref[:, 0:2 * d_a], preferred_element_type=F32)

    bias_b = bdw_ref[...]
    out_pitch = tm + CV_PITCH_PAD
    for t0 in range(0, tm, CONV_TOKENS):
        accs = [bias_b] * CONV_TOKENS
        for k0 in range(0, CONV_WIDTH, CONV_TAP_GROUP):
            k1 = min(k0 + CONV_TAP_GROUP, CONV_WIDTH)
            rows = {r: xc_ref[pl.ds(r + HIST_OFF, n_slab, stride=pitch), :]
                    for r in range(t0 + k0, t0 + CONV_TOKENS + k1 - 1)}
            for k in range(k0, k1):
                w_k = wdw_ref[k]
                accs = [accs[i] + w_k * rows[t0 + i + k] for i in range(CONV_TOKENS)]
        for i in range(CONV_TOKENS):
            cv_ref[pl.ds(t0 + i, n_slab, stride=out_pitch), :] = accs[i]

    for j in range(n_slab):
        xc_ref[j * pitch:j * pitch + HIST_PAD, :] = xc_ref[j * pitch + tm:j * pitch + tm + HIST_PAD, :]

    row_i = lax.broadcasted_iota(jnp.int32, (chunk_len, chunk_len), 0)
    col_j = lax.broadcasted_iota(jnp.int32, (chunk_len, chunk_len), 1)
    tril = row_i >= col_j
    ws = [jnp.where(tril, ws_ref[g, 0:chunk_len, 0:chunk_len], 0.0).astype(BF16)
          for g in range(N_GROUPS_A)]
    for c in range(tm // chunk_len):
        rows = pl.ds(c * chunk_len, chunk_len)
        vn = _rms(za_ref[rows, d_a:2 * d_a], gv_ref[...])
        if emit_v:
            v_ref[0, rows, :] = vn
        vb = vn.astype(BF16)
        sp = jnp.concatenate(
            [jnp.dot(ws[g], vb[:, g * HEAD_DIM_A:(g + 1) * HEAD_DIM_A],
                     preferred_element_type=F32) for g in range(N_GROUPS_A)], axis=1)
        ya = za_ref[rows, 0:d_a] * (sp + bias_ref[0:chunk_len, :])
        ycat_ref[rows, 0:d_a] = _rms(ya, goa_ref[...]).astype(BF16)

    cv = jnp.concatenate([cv_ref[j * out_pitch:j * out_pitch + tm, :] for j in range(n_slab)],
                         axis=1)
    mu = jnp.mean(cv, axis=-1, keepdims=True)
    cc = cv - mu
    ln = cc * lax.rsqrt(jnp.mean(cc * cc, axis=-1, keepdims=True) + EPS) * gcn_ref[...] + bcn_ref[...]
    ycat_ref[:, d_a:d_a + d_b] = _rms(_silu(ln), gob_ref[...]).astype(BF16)

    y = jnp.dot(ycat_ref[...], wout_ref[...], preferred_element_type=F32)
    o_ref[...] = x_ref[...] + gt * y


def _mixing(x, mod, hist, g_mix, w_in, g_v, w_s, bias, w_dw, b_dw, g_cn, b_cn, g_out_a, g_out_b,
            w_out, *, batch, tm, chunk_len, emit_v):
    t, d = x.shape
    seq = t // batch
    nt = seq // tm
    d_a = g_v.shape[0]
    d_b = g_out_b.shape[0]
    n_slab = d_b // LANES
    const2 = lambda b, i: (0, 0)
    single = pl.Buffered(1)
    kern = functools.partial(_mix_kernel, chunk_len=chunk_len, emit_v=emit_v)
    out_shape = [jax.ShapeDtypeStruct((t, d), F32),
                 jax.ShapeDtypeStruct((batch, HIST, d_b), F32)]
    out_specs = [pl.BlockSpec((tm, d), lambda b, i: (b * nt + i, 0)),
                 pl.BlockSpec((1, HIST, d_b), lambda b, i: (b, 0, 0))]
    if emit_v:
        out_shape.append(jax.ShapeDtypeStruct((batch, seq, d_a), F32))
        out_specs.append(pl.BlockSpec((1, tm, d_a), lambda b, i: (b, i, 0)))
    return pl.pallas_call(
        kern,
        grid=(batch, nt),
        in_specs=[
            pl.BlockSpec((tm, d), lambda b, i: (b * nt + i, 0)),
            pl.BlockSpec((1, N_MOD, d), lambda b, i: (b, 0, 0)),
            pl.BlockSpec((1, d), const2),
            pl.BlockSpec(w_in.shape, const2, pipeline_mode=single),
            pl.BlockSpec((1, d_a), const2),
            pl.BlockSpec(w_s.shape, lambda b, i: (0, 0, 0)),
            pl.BlockSpec(bias.shape, const2),
            pl.BlockSpec((1, HIST_PAD, d_b), lambda b, i: (b, 0, 0)),
            pl.BlockSpec((w_dw.shape[0], n_slab, LANES), lambda b, i: (0, 0, 0)),
            pl.BlockSpec((n_slab, LANES), const2),
            pl.BlockSpec((1, d_b), const2),
            pl.BlockSpec((1, d_b), const2),
            pl.BlockSpec((1, d_a), const2),
            pl.BlockSpec((1, d_b), const2),
            pl.BlockSpec(w_out.shape, const2, pipeline_mode=single),
        ],
        out_specs=out_specs,
        out_shape=out_shape,
        scratch_shapes=[
            pltpu.VMEM((tm, d), BF16),
            pltpu.VMEM((tm, 2 * d_a), F32),
            pltpu.VMEM((tm, 2 * d_b), F32),
            pltpu.VMEM((tm, d_a + d_b), BF16),
            pltpu.VMEM((n_slab * (HIST_PAD + tm + XC_PITCH_PAD), LANES), F32),
            pltpu.VMEM((n_slab * (tm + CV_PITCH_PAD), LANES), F32),
        ],
        compiler_params=pltpu.CompilerParams(
            dimension_semantics=("arbitrary", "arbitrary"), vmem_limit_bytes=VMEM_LIMIT_BYTES),
        name="mixing_sublayer",
    )(x, mod, g_mix.reshape(1, d), w_in, g_v.reshape(1, d_a), w_s, bias, hist,
      w_dw.reshape(w_dw.shape[0], n_slab, LANES), b_dw.reshape(n_slab, LANES),
      g_cn.reshape(1, d_b), b_cn.reshape(1, d_b), g_out_a.reshape(1, d_a),
      g_out_b.reshape(1, d_b), w_out)


def kernel(x_prompt, x_sample, cache_conv, c_prompt, c_sample, w_ada, b_ada, g_ffn1, w_up1, w_down1, g_mix, w_in, g_v, w_s, b_s, w_dw, b_dw, g_cn, b_cn, g_out_a, g_out_b, w_out, g_ffn2, w_up2, w_down2, g_final):
    depth = w_ada.shape[0]
    bp, seq, d = x_prompt.shape
    bs, dec_seq, _ = x_sample.shape
    d_b = w_dw.shape[-1]
    mlp_chunk = w_s.shape[-1]

    xp = x_prompt.reshape(bp * seq, d)
    xs = x_sample.reshape(bs * dec_seq, d)
    c_all = jnp.concatenate([c_prompt, c_sample], axis=0)
    hist_p = jnp.zeros((bp, HIST_PAD, d_b), F32)

    conv_p, conv_s, v_s = [], [], []
    for l in range(depth):
        mod = _modulation(c_all, w_ada[l], b_ada[l]).reshape(bp + bs, N_MOD, d)
        mod_p, mod_s = mod[:bp], mod[bp:]
        wu1, wd1 = w_up1[l].astype(BF16), w_down1[l].astype(BF16)
        wu2, wd2 = w_up2[l].astype(BF16), w_down2[l].astype(BF16)
        wi, wo = w_in[l].astype(BF16), w_out[l].astype(BF16)
        bias = jnp.repeat(b_s[l].T, HEAD_DIM_A, axis=1)
        hist_s = jnp.pad(cache_conv[l], ((0, 0), (HIST_OFF, 0), (0, 0)))
        last = l == depth - 1
        mix_w = (g_mix[l], wi, g_v[l], w_s[l], bias, w_dw[l], b_dw[l], g_cn[l], b_cn[l],
                 g_out_a[l], g_out_b[l], wo)
        prompt_tiles = dict(tm=FFN_TM, tf=FFN_TF, seg_rows=FFN_TM, tiles_per_mod=seq // FFN_TM)
        sample_tiles = dict(tm=bs * dec_seq, tf=_hidden_tile(wd1.shape[0], FFN_TF_SAMPLE_MAX),
                            seg_rows=dec_seq, tiles_per_mod=1)

        xp = _ffn(xp, mod_p, g_ffn1[l], wu1, wd1, g_final, mod_base=0, final_norm=False,
                  **prompt_tiles)
        xs = _ffn(xs, mod_s, g_ffn1[l], wu1, wd1, g_final, mod_base=0, final_norm=False,
                  **sample_tiles)

        xp, cp = _mixing(xp, mod_p, hist_p, *mix_w, batch=bp, tm=MIX_TM, chunk_len=mlp_chunk,
                         emit_v=False)
        xs, cs, vs = _mixing(xs, mod_s, hist_s, *mix_w, batch=bs, tm=dec_seq,
                             chunk_len=dec_seq, emit_v=True)

        xp = _ffn(xp, mod_p, g_ffn2[l], wu2, wd2, g_final, mod_base=6, final_norm=last,
                  **prompt_tiles)
        xs = _ffn(xs, mod_s, g_ffn2[l], wu2, wd2, g_final, mod_base=6, final_norm=last,
                  **sample_tiles)
        conv_p.append(cp)
        conv_s.append(cs)
        v_s.append(vs)

    if depth == 0:
        raise ValueError("depth must be at least 1")
    return (xp.reshape(bp, seq, d), xs.reshape(bs, dec_seq, d), jnp.stack(conv_p),
            jnp.stack(conv_s), jnp.stack(v_s))
```

```python
import functools

import jax
import jax.numpy as jnp
from jax import lax
from jax.experimental import pallas as pl
from jax.experimental.pallas import tpu as pltpu

EPS = 1e-6
N_MOD = 9
N_GROUPS_A = 8
HEAD_DIM_A = 128
CONV_WIDTH = 31
HIST = CONV_WIDTH - 1
HIST_PAD = 32
HIST_OFF = HIST_PAD - HIST

F32 = jnp.float32
BF16 = jnp.bfloat16

VMEM_LIMIT_BYTES = 56 * 1024 * 1024

MOD_TN = 1024
FFN_TM = 1024
FFN_TF = 512
FFN_DOWN_SPLIT = 2
FFN_PROLOGUE_ROWS = 64
FFN_OPEN_PARTS = 2
MIX_TM = 256
LANES = 128
XC_PITCH_PAD = 4
CV_PITCH_PAD = 8
CONV_TOKENS = 8
CONV_TAP_GROUP = 16


def _sigmoid(x):
    return 1.0 / (1.0 + jnp.exp(-x))


def _silu(x):
    return x * _sigmoid(x)


def _rms(x, g):
    return (x * lax.rsqrt(jnp.mean(x * x, axis=-1, keepdims=True) + EPS)) * g


def _mod_kernel(c_ref, w_ref, b_ref, o_ref):
    s = _silu(c_ref[...]).astype(BF16)
    o_ref[...] = jnp.dot(s, w_ref[...].astype(BF16), preferred_element_type=F32) + b_ref[...]


def _modulation(c, w_ada, b_ada):
    rows, d = c.shape
    n = w_ada.shape[1]
    return pl.pallas_call(
        _mod_kernel,
        grid=(n // MOD_TN,),
        in_specs=[
            pl.BlockSpec((rows, d), lambda j: (0, 0)),
            pl.BlockSpec((d, MOD_TN), lambda j: (0, j)),
            pl.BlockSpec((1, MOD_TN), lambda j: (0, j)),
        ],
        out_specs=pl.BlockSpec((rows, MOD_TN), lambda j: (0, j)),
        out_shape=jax.ShapeDtypeStruct((rows, n), F32),
        compiler_params=pltpu.CompilerParams(
            dimension_semantics=("arbitrary",), vmem_limit_bytes=VMEM_LIMIT_BYTES),
        name="modulation",
    )(c, w_ada, b_ada.reshape(1, n))


def _ffn_kernel(x_hbm, mod_ref, g_ref, *refs, mod_base, seg_rows, final_norm, cast_weights):
    if cast_weights:
        wg32_ref, wu32_ref, wd32_ref, gf_ref, o_ref, wgu_ref, wd_ref, x_buf, h_ref, x_sem = refs
        tf = wd_ref.shape[0]
        wgu_ref[:, 0:tf] = wg32_ref[...].astype(BF16)
        wgu_ref[:, tf:2 * tf] = wu32_ref[...].astype(BF16)
        wd_ref[...] = wd32_ref[...].astype(BF16)
    else:
        wgu_ref, wd_ref, gf_ref, o_ref, x_buf, h_ref, x_sem = refs
        tf = wd_ref.shape[0]
    i = pl.program_id(0)
    j = pl.program_id(1)
    tm, d = o_ref.shape
    n_seg = mod_ref.shape[0]

    def x_copy(tile):
        return pltpu.make_async_copy(x_hbm.at[pl.ds(tile * tm, tm), :], x_buf, x_sem)

    @pl.when(jnp.logical_and(i == 0, j == 0))
    def _():
        x_copy(0).start()

    def swiglu_rows(r0, n_rows, first):
        h = h_ref[pl.ds(r0, n_rows), :]
        gu = jnp.dot(h, wgu_ref[...], preferred_element_type=F32)
        a = (_silu(gu[:, 0:tf]) * gu[:, tf:2 * tf]).astype(BF16)
        for n in range(FFN_DOWN_SPLIT):
            cols = pl.ds(n * (d // FFN_DOWN_SPLIT), d // FFN_DOWN_SPLIT)
            y = jnp.dot(a, wd_ref[:, cols], preferred_element_type=F32)
            start = r0
            while start < r0 + n_rows:
                s = start // seg_rows
                stop = min((s + 1) * seg_rows, r0 + n_rows)
                rows = pl.ds(start, stop - start)
                half_gate = 0.5 * mod_ref[s, mod_base + 2:mod_base + 3, cols]
                update = half_gate * y[start - r0:stop - r0, :]
                if first:
                    o_ref[rows, cols] = x_buf[rows, cols] + update
                else:
                    o_ref[rows, cols] += update
                start = stop

    def open_rows(s, r0, n_rows):
        sh = mod_ref[s, mod_base:mod_base + 1, :]
        scale = g_ref[...] * (1.0 + mod_ref[s, mod_base + 1:mod_base + 2, :])
        chunk = min(n_rows, FFN_PROLOGUE_ROWS)
        for c in range(n_rows // chunk):
            rows = pl.ds(r0 + c * chunk, chunk)
            x = x_buf[rows, :]
            inv = lax.rsqrt(jnp.mean(x * x, axis=-1, keepdims=True) + EPS)
            h_ref[rows, :] = ((x * inv) * scale + sh).astype(BF16)

    @pl.when(j == 0)
    def _():
        x_copy(i).wait()
        if n_seg == 1:
            part = tm // FFN_OPEN_PARTS
            for p in range(FFN_OPEN_PARTS):
                open_rows(0, p * part, part)
                swiglu_rows(p * part, part, first=True)
        else:
            for s in range(n_seg):
                open_rows(s, s * seg_rows, seg_rows)
            swiglu_rows(0, tm, first=True)

    @pl.when(jnp.logical_and(j == 1, i + 1 < pl.num_programs(0)))
    def _():
        x_copy(i + 1).start()

    @pl.when(j > 0)
    def _():
        swiglu_rows(0, tm, first=False)

    if final_norm:
        @pl.when(j == pl.num_programs(1) - 1)
        def _():
            o_ref[...] = _rms(o_ref[...], gf_ref[...])


def _ffn(x, mod, g, w_up, w_down, g_final, *, tm, tf, seg_rows, tiles_per_mod, mod_base,
         final_norm, cast_weights=False):
    t, d = x.shape
    f = w_down.shape[0]
    nf = f // tf
    assert nf >= 2 and f % tf == 0 and t % tm == 0 and tm % seg_rows == 0
    assert not cast_weights or t == tm
    n_seg = tm // seg_rows
    kern = functools.partial(_ffn_kernel, mod_base=mod_base, seg_rows=seg_rows,
                             final_norm=final_norm, cast_weights=cast_weights)
    if cast_weights:
        weight_specs = [pl.BlockSpec((d, tf), lambda i, j: (0, j)),
                        pl.BlockSpec((d, tf), lambda i, j: (0, nf + j)),
                        pl.BlockSpec((tf, d), lambda i, j: (j, 0))]
        weights = (w_up, w_up, w_down)
    else:
        weight_specs = [pl.BlockSpec((d, 2 * tf), lambda i, j: (0, j)),
                        pl.BlockSpec((tf, d), lambda i, j: (j, 0))]
        weights = (w_up, w_down)
    out_specs = [pl.BlockSpec((tm, d), lambda i, j: (i, 0))]
    out_shape = [jax.ShapeDtypeStruct((t, d), F32)]
    if cast_weights:
        out_specs += [pl.BlockSpec((d, 2 * tf), lambda i, j: (0, j)),
                      pl.BlockSpec((tf, d), lambda i, j: (j, 0))]
        out_shape += [jax.ShapeDtypeStruct((d, 2 * f), BF16), jax.ShapeDtypeStruct((f, d), BF16)]
    out = pl.pallas_call(
        kern,
        grid=(t // tm, nf),
        in_specs=[
            pl.BlockSpec(memory_space=pl.ANY),
            pl.BlockSpec((n_seg, N_MOD, d), lambda i, j: (i // tiles_per_mod, 0, 0)),
            pl.BlockSpec((1, d), lambda i, j: (0, 0)),
            *weight_specs,
            pl.BlockSpec((1, d), lambda i, j: (0, 0)),
        ],
        out_specs=out_specs,
        out_shape=out_shape,
        scratch_shapes=[pltpu.VMEM((tm, d), F32), pltpu.VMEM((tm, d), BF16),
                        pltpu.SemaphoreType.DMA(())],
        compiler_params=pltpu.CompilerParams(
            dimension_semantics=("arbitrary", "arbitrary"), vmem_limit_bytes=VMEM_LIMIT_BYTES),
        name="swiglu_half_step",
    )(x, mod, g.reshape(1, d), *weights, g_final.reshape(1, d))
    return out if cast_weights else out[0]


def _mix_kernel(x_ref, mod_ref, gmix_ref, win_ref, gv_ref, ws_ref, bias_ref, hist_ref, wdw_ref,
                bdw_ref, gcn_ref, bcn_ref, goa_ref, gob_ref, wout_ref,
                o_ref, conv_ref, *rest, chunk_len, emit_v):
    if emit_v:
        v_ref, h_ref, za_ref, zb_ref, ycat_ref, xc_ref, cv_ref = rest
    else:
        v_ref = None
        h_ref, za_ref, zb_ref, ycat_ref, xc_ref, cv_ref = rest
    tm = x_ref.shape[0]
    d_a = gv_ref.shape[1]
    d_b = gob_ref.shape[1]
    n_slab = d_b // LANES
    pitch = HIST_PAD + tm + XC_PITCH_PAD
    t = pl.program_id(1)

    @pl.when(t == 0)
    def _():
        for j in range(n_slab):
            xc_ref[j * pitch:j * pitch + HIST_PAD, :] = hist_ref[0, :, j * LANES:(j + 1) * LANES]

    sh = mod_ref[0, 3:4, :]
    sc = mod_ref[0, 4:5, :]
    gt = mod_ref[0, 5:6, :]
    h_ref[...] = (_rms(x_ref[...], gmix_ref[...]) * (1.0 + sc) + sh).astype(BF16)

    zb_ref[...] = jnp.dot(h_ref[...], win_ref[:, 2 * d_a:2 * d_a + 2 * d_b],
                          preferred_element_type=F32)
    glu = zb_ref[:, 0:d_b] * _sigmoid(zb_ref[:, d_b:2 * d_b])
    for j in range(n_slab):
        xc_ref[j * pitch + HIST_PAD:j * pitch + HIST_PAD + tm, :] = glu[:, j * LANES:(j + 1) * LANES]
        conv_ref[0, :, j * LANES:(j + 1) * LANES] = (
            xc_ref[j * pitch + tm + HIST_OFF:j * pitch + tm + HIST_PAD, :])

    za_ref[...] = jnp.dot(h_ref[...], win_ref[:, 0:2 * d_a], preferred_element_type=F32)

    bias_b = bdw_ref[...]
    out_pitch = tm + CV_PITCH_PAD
    for t0 in range(0, tm, CONV_TOKENS):
        accs = [bias_b] * CONV_TOKENS
        for k0 in range(0, CONV_WIDTH, CONV_TAP_GROUP):
            k1 = min(k0 + CONV_TAP_GROUP, CONV_WIDTH)
            rows = {r: xc_ref[pl.ds(r + HIST_OFF, n_slab, stride=pitch), :]
                    for r in range(t0 + k0, t0 + CONV_TOKENS + k1 - 1)}
            for k in range(k0, k1):
                w_k = wdw_ref[k]
                accs = [accs[i] + w_k * rows[t0 + i + k] for i in range(CONV_TOKENS)]
        for i in range(CONV_TOKENS):
            cv_ref[pl.ds(t0 + i, n_slab, stride=out_pitch), :] = accs[i]

    for j in range(n_slab):
        xc_ref[j * pitch:j * pitch + HIST_PAD, :] = xc_ref[j * pitch + tm:j * pitch + tm + HIST_PAD, :]

    row_i = lax.broadcasted_iota(jnp.int32, (chunk_len, chunk_len), 0)
    col_j = lax.broadcasted_iota(jnp.int32, (chunk_len, chunk_len), 1)
    tril = row_i >= col_j
    ws = [jnp.where(tril, ws_ref[g, 0:chunk_len, 0:chunk_len], 0.0).astype(BF16)
          for g in range(N_GROUPS_A)]
    for c in range(tm // chunk_len):
        rows = pl.ds(c * chunk_len, chunk_len)
        vn = _rms(za_ref[rows, d_a:2 * d_a], gv_ref[...])
        if emit_v:
            v_ref[0, rows, :] = vn
        vb = vn.astype(BF16)
        sp = jnp.concatenate(
            [jnp.dot(ws[g], vb[:, g * HEAD_DIM_A:(g + 1) * HEAD_DIM_A],
                     preferred_element_type=F32) for g in range(N_GROUPS_A)], axis=1)
        ya = za_ref[rows, 0:d_a] * (sp + bias_ref[0:chunk_len, :])
        ycat_ref[rows, 0:d_a] = _rms(ya, goa_ref[...]).astype(BF16)

    cv = jnp.concatenate([cv_ref[j * out_pitch:j * out_pitch + tm, :] for j in range(n_slab)],
                         axis=1)
    mu = jnp.mean(cv, axis=-1, keepdims=True)
    cc = cv - mu
    ln = cc * lax.rsqrt(jnp.mean(cc * cc, axis=-1, keepdims=True) + EPS) * gcn_ref[...] + bcn_ref[...]
    ycat_ref[:, d_a:d_a + d_b] = _rms(_silu(ln), gob_ref[...]).astype(BF16)

    y = jnp.dot(ycat_ref[...], wout_ref[...], preferred_element_type=F32)
    o_ref[...] = x_ref[...] + gt * y


def _mixing(x, mod, hist, g_mix, w_in, g_v, w_s, bias, w_dw, b_dw, g_cn, b_cn, g_out_a, g_out_b,
            w_out, *, batch, tm, chunk_len, emit_v):
    t, d = x.shape
    seq = t // batch
    nt = seq // tm
    d_a = g_v.shape[0]
    d_b = g_out_b.shape[0]
    n_slab = d_b // LANES
    const2 = lambda b, i: (0, 0)
    single = pl.Buffered(1)
    kern = functools.partial(_mix_kernel, chunk_len=chunk_len, emit_v=emit_v)
    out_shape = [jax.ShapeDtypeStruct((t, d), F32),
                 jax.ShapeDtypeStruct((batch, HIST, d_b), F32)]
    out_specs = [pl.BlockSpec((tm, d), lambda b, i: (b * nt + i, 0)),
                 pl.BlockSpec((1, HIST, d_b), lambda b, i: (b, 0, 0))]
    if emit_v:
        out_shape.append(jax.ShapeDtypeStruct((batch, seq, d_a), F32))
        out_specs.append(pl.BlockSpec((1, tm, d_a), lambda b, i: (b, i, 0)))
    return pl.pallas_call(
        kern,
        grid=(batch, nt),
        in_specs=[
            pl.BlockSpec((tm, d), lambda b, i: (b * nt + i, 0)),
            pl.BlockSpec((1, N_MOD, d), lambda b, i: (b, 0, 0)),
            pl.BlockSpec((1, d), const2),
            pl.BlockSpec(w_in.shape, const2, pipeline_mode=single),
            pl.BlockSpec((1, d_a), const2),
            pl.BlockSpec(w_s.shape, lambda b, i: (0, 0, 0)),
            pl.BlockSpec(bias.shape, const2),
            pl.BlockSpec((1, HIST_PAD, d_b), lambda b, i: (b, 0, 0)),
            pl.BlockSpec((w_dw.shape[0], n_slab, LANES), lambda b, i: (0, 0, 0)),
            pl.BlockSpec((n_slab, LANES), const2),
            pl.BlockSpec((1, d_b), const2),
            pl.BlockSpec((1, d_b), const2),
            pl.BlockSpec((1, d_a), const2),
            pl.BlockSpec((1, d_b), const2),
            pl.BlockSpec(w_out.shape, const2, pipeline_mode=single),
        ],
        out_specs=out_specs,
        out_shape=out_shape,
        scratch_shapes=[
            pltpu.VMEM((tm, d), BF16),
            pltpu.VMEM((tm, 2 * d_a), F32),
            pltpu.VMEM((tm, 2 * d_b), F32),
            pltpu.VMEM((tm, d_a + d_b), BF16),
            pltpu.VMEM((n_slab * (HIST_PAD + tm + XC_PITCH_PAD), LANES), F32),
            pltpu.VMEM((n_slab * (tm + CV_PITCH_PAD), LANES), F32),
        ],
        compiler_params=pltpu.CompilerParams(
            dimension_semantics=("arbitrary", "arbitrary"), vmem_limit_bytes=VMEM_LIMIT_BYTES),
        name="mixing_sublayer",
    )(x, mod, g_mix.reshape(1, d), w_in, g_v.reshape(1, d_a), w_s, bias, hist,
      w_dw.reshape(w_dw.shape[0], n_slab, LANES), b_dw.reshape(n_slab, LANES),
      g_cn.reshape(1, d_b), b_cn.reshape(1, d_b), g_out_a.reshape(1, d_a),
      g_out_b.reshape(1, d_b), w_out)


def kernel(x_prompt, x_sample, cache_conv, c_prompt, c_sample, w_ada, b_ada, g_ffn1, w_up1, w_down1, g_mix, w_in, g_v, w_s, b_s, w_dw, b_dw, g_cn, b_cn, g_out_a, g_out_b, w_out, g_ffn2, w_up2, w_down2, g_final):
    depth = w_ada.shape[0]
    bp, seq, d = x_prompt.shape
    bs, dec_seq, _ = x_sample.shape
    d_b = w_dw.shape[-1]
    mlp_chunk = w_s.shape[-1]

    xp = x_prompt.reshape(bp * seq, d)
    xs = x_sample.reshape(bs * dec_seq, d)
    c_all = jnp.concatenate([c_prompt, c_sample], axis=0)
    hist_p = jnp.zeros((bp, HIST_PAD, d_b), F32)

    conv_p, conv_s, v_s = [], [], []
    for l in range(depth):
        mod = _modulation(c_all, w_ada[l], b_ada[l]).reshape(bp + bs, N_MOD, d)
        mod_p, mod_s = mod[:bp], mod[bp:]
        wi, wo = w_in[l].astype(BF16), w_out[l].astype(BF16)
        bias = jnp.repeat(b_s[l].T, HEAD_DIM_A, axis=1)
        hist_s = jnp.pad(cache_conv[l], ((0, 0), (HIST_OFF, 0), (0, 0)))
        last = l == depth - 1
        mix_w = (g_mix[l], wi, g_v[l], w_s[l], bias, w_dw[l], b_dw[l], g_cn[l], b_cn[l],
                 g_out_a[l], g_out_b[l], wo)
        prompt_tiles = dict(tm=FFN_TM, tf=FFN_TF, seg_rows=FFN_TM, tiles_per_mod=seq // FFN_TM)
        sample_tiles = dict(tm=bs * dec_seq, tf=FFN_TF, seg_rows=dec_seq, tiles_per_mod=1)

        xs, wu1, wd1 = _ffn(xs, mod_s, g_ffn1[l], w_up1[l], w_down1[l], g_final, mod_base=0,
                            final_norm=False, cast_weights=True, **sample_tiles)
        xp = _ffn(xp, mod_p, g_ffn1[l], wu1, wd1, g_final, mod_base=0, final_norm=False,
                  **prompt_tiles)

        xp, cp = _mixing(xp, mod_p, hist_p, *mix_w, batch=bp, tm=MIX_TM, chunk_len=mlp_chunk,
                         emit_v=False)
        xs, cs, vs = _mixing(xs, mod_s, hist_s, *mix_w, batch=bs, tm=dec_seq,
                             chunk_len=dec_seq, emit_v=True)

        xs, wu2, wd2 = _ffn(xs, mod_s, g_ffn2[l], w_up2[l], w_down2[l], g_final, mod_base=6,
                            final_norm=last, cast_weights=True, **sample_tiles)
        xp = _ffn(xp, mod_p, g_ffn2[l], wu2, wd2, g_final, mod_base=6, final_norm=last,
                  **prompt_tiles)
        conv_p.append(cp)
        conv_s.append(cs)
        v_s.append(vs)

    if depth == 0:
        raise ValueError("depth must be at least 1")
    return (xp.reshape(bp, seq, d), xs.reshape(bs, dec_seq, d), jnp.stack(conv_p),
            jnp.stack(conv_s), jnp.stack(v_s))
```

```python
import functools

import jax
import jax.numpy as jnp
from jax import lax
from jax.experimental import pallas as pl
from jax.experimental.pallas import tpu as pltpu

EPS = 1e-6
N_MOD = 9
N_GROUPS_A = 8
HEAD_DIM_A = 128
CONV_WIDTH = 31
HIST = CONV_WIDTH - 1
HIST_PAD = 32
HIST_OFF = HIST_PAD - HIST

F32 = jnp.float32
BF16 = jnp.bfloat16

VMEM_LIMIT_BYTES = 56 * 1024 * 1024

MOD_TN = 1024
FFN_TM = 1024
FFN_TF = 512
FFN_DOWN_SPLIT = 2
FFN_PROLOGUE_ROWS = 64
FFN_OPEN_PARTS = 2
MIX_TM = 256
LANES = 128
XC_PITCH_PAD = 4
CV_PITCH_PAD = 8
CONV_TOKENS = 8
CONV_TAP_GROUP = 16


def _sigmoid(x):
    return 1.0 / (1.0 + jnp.exp(-x))


def _silu(x):
    return x * _sigmoid(x)


def _rms(x, g):
    return (x * lax.rsqrt(jnp.mean(x * x, axis=-1, keepdims=True) + EPS)) * g


def _mod_kernel(c_ref, w_ref, b_ref, o_ref):
    s = _silu(c_ref[...]).astype(BF16)
    o_ref[...] = jnp.dot(s, w_ref[...].astype(BF16), preferred_element_type=F32) + b_ref[...]


def _modulation(c, w_ada, b_ada):
    rows, d = c.shape
    n = w_ada.shape[1]
    return pl.pallas_call(
        _mod_kernel,
        grid=(n // MOD_TN,),
        in_specs=[
            pl.BlockSpec((rows, d), lambda j: (0, 0)),
            pl.BlockSpec((d, MOD_TN), lambda j: (0, j)),
            pl.BlockSpec((1, MOD_TN), lambda j: (0, j)),
        ],
        out_specs=pl.BlockSpec((rows, MOD_TN), lambda j: (0, j)),
        out_shape=jax.ShapeDtypeStruct((rows, n), F32),
        compiler_params=pltpu.CompilerParams(
            dimension_semantics=("arbitrary",), vmem_limit_bytes=VMEM_LIMIT_BYTES),
        name="modulation",
    )(c, w_ada, b_ada.reshape(1, n))


def _ffn_kernel(x_hbm, mod_ref, g_ref, *refs, mod_base, seg_rows, final_norm, cast_weights):
    if cast_weights:
        wg32_ref, wu32_ref, wd32_ref, gf_ref, o_ref, wgu_ref, wd_ref, x_buf, h_ref, x_sem = refs
        tf = wd_ref.shape[0]
        wgu_ref[:, 0:tf] = wg32_ref[...].astype(BF16)
        wgu_ref[:, tf:2 * tf] = wu32_ref[...].astype(BF16)
        wd_ref[...] = wd32_ref[...].astype(BF16)
    else:
        wgu_ref, wd_ref, gf_ref, o_ref, x_buf, h_ref, x_sem = refs
        tf = wd_ref.shape[0]
    i = pl.program_id(0)
    j = pl.program_id(1)
    tm, d = o_ref.shape
    n_seg = mod_ref.shape[0]

    def x_copy(tile):
        return pltpu.make_async_copy(x_hbm.at[pl.ds(tile * tm, tm), :], x_buf, x_sem)

    @pl.when(jnp.logical_and(i == 0, j == 0))
    def _():
        x_copy(0).start()

    def swiglu_rows(r0, n_rows, first):
        h = h_ref[pl.ds(r0, n_rows), :]
        gu = jnp.dot(h, wgu_ref[...], preferred_element_type=F32)
        a = (_silu(gu[:, 0:tf]) * gu[:, tf:2 * tf]).astype(BF16)
        for n in range(FFN_DOWN_SPLIT):
            cols = pl.ds(n * (d // FFN_DOWN_SPLIT), d // FFN_DOWN_SPLIT)
            y = jnp.dot(a, wd_ref[:, cols], preferred_element_type=F32)
            start = r0
            while start < r0 + n_rows:
                s = start // seg_rows
                stop = min((s + 1) * seg_rows, r0 + n_rows)
                rows = pl.ds(start, stop - start)
                half_gate = 0.5 * mod_ref[s, mod_base + 2:mod_base + 3, cols]
                update = half_gate * y[start - r0:stop - r0, :]
                if first:
                    o_ref[rows, cols] = x_buf[rows, cols] + update
                else:
                    o_ref[rows, cols] += update
                start = stop

    def open_rows(s, r0, n_rows):
        sh = mod_ref[s, mod_base:mod_base + 1, :]
        scale = g_ref[...] * (1.0 + mod_ref[s, mod_base + 1:mod_base + 2, :])
        chunk = min(n_rows, FFN_PROLOGUE_ROWS)
        for c in range(n_rows // chunk):
            rows = pl.ds(r0 + c * chunk, chunk)
            x = x_buf[rows, :]
            inv = lax.rsqrt(jnp.mean(x * x, axis=-1, keepdims=True) + EPS)
            h_ref[rows, :] = ((x * inv) * scale + sh).astype(BF16)

    @pl.when(j == 0)
    def _():
        x_copy(i).wait()
        if n_seg == 1:
            part = tm // FFN_OPEN_PARTS
            for p in range(FFN_OPEN_PARTS):
                open_rows(0, p * part, part)
                swiglu_rows(p * part, part, first=True)
        else:
            for s in range(n_seg):
                open_rows(s, s * seg_rows, seg_rows)
            swiglu_rows(0, tm, first=True)

    @pl.when(jnp.logical_and(j == 1, i + 1 < pl.num_programs(0)))
    def _():
        x_copy(i + 1).start()

    @pl.when(j > 0)
    def _():
        swiglu_rows(0, tm, first=False)

    if final_norm:
        @pl.when(j == pl.num_programs(1) - 1)
        def _():
            o_ref[...] = _rms(o_ref[...], gf_ref[...])


def _ffn(x, mod, g, w_up, w_down, g_final, *, tm, tf, seg_rows, tiles_per_mod, mod_base,
         final_norm, cast_weights=False):
    t, d = x.shape
    f = w_down.shape[0]
    nf = f // tf
    assert nf >= 2 and f % tf == 0 and t % tm == 0 and tm % seg_rows == 0
    assert not cast_weights or t == tm
    n_seg = tm // seg_rows
    kern = functools.partial(_ffn_kernel, mod_base=mod_base, seg_rows=seg_rows,
                             final_norm=final_norm, cast_weights=cast_weights)
    if cast_weights:
        weight_specs = [pl.BlockSpec((d, tf), lambda i, j: (0, j)),
                        pl.BlockSpec((d, tf), lambda i, j: (0, nf + j)),
                        pl.BlockSpec((tf, d), lambda i, j: (j, 0))]
        weights = (w_up, w_up, w_down)
    else:
        weight_specs = [pl.BlockSpec((d, 2 * tf), lambda i, j: (0, j)),
                        pl.BlockSpec((tf, d), lambda i, j: (j, 0))]
        weights = (w_up, w_down)
    out_specs = [pl.BlockSpec((tm, d), lambda i, j: (i, 0))]
    out_shape = [jax.ShapeDtypeStruct((t, d), F32)]
    if cast_weights:
        out_specs += [pl.BlockSpec((d, 2 * tf), lambda i, j: (0, j)),
                      pl.BlockSpec((tf, d), lambda i, j: (j, 0))]
        out_shape += [jax.ShapeDtypeStruct((d, 2 * f), BF16), jax.ShapeDtypeStruct((f, d), BF16)]
    out = pl.pallas_call(
        kern,
        grid=(t // tm, nf),
        in_specs=[
            pl.BlockSpec(memory_space=pl.ANY),
            pl.BlockSpec((n_seg, N_MOD, d), lambda i, j: (i // tiles_per_mod, 0, 0)),
            pl.BlockSpec((1, d), lambda i, j: (0, 0)),
            *weight_specs,
            pl.BlockSpec((1, d), lambda i, j: (0, 0)),
        ],
        out_specs=out_specs,
        out_shape=out_shape,
        scratch_shapes=[pltpu.VMEM((tm, d), F32), pltpu.VMEM((tm, d), BF16),
                        pltpu.SemaphoreType.DMA(())],
        compiler_params=pltpu.CompilerParams(
            dimension_semantics=("arbitrary", "arbitrary"), vmem_limit_bytes=VMEM_LIMIT_BYTES),
        name="swiglu_half_step",
    )(x, mod, g.reshape(1, d), *weights, g_final.reshape(1, d))
    return out if cast_weights else out[0]


def _mix_kernel(x_ref, mod_ref, gmix_ref, win_hbm, gv_ref, ws_ref, bias_ref, hist_ref, wdw_ref,
                bdw_ref, gcn_ref, bcn_ref, goa_ref, gob_ref, wout_hbm,
                o_ref, conv_ref, *rest, chunk_len, emit_v):
    if emit_v:
        v_ref, h_ref, za_ref, zb_ref, ycat_ref, xc_ref, cv_ref, win_ref, wout_ref, w_sem = rest
    else:
        v_ref = None
        h_ref, za_ref, zb_ref, ycat_ref, xc_ref, cv_ref, win_ref, wout_ref, w_sem = rest
    tm = x_ref.shape[0]
    d_a = gv_ref.shape[1]
    d_b = gob_ref.shape[1]
    n_slab = d_b // LANES
    pitch = HIST_PAD + tm + XC_PITCH_PAD
    t = pl.program_id(1)

    @pl.when(jnp.logical_and(pl.program_id(0) == 0, t == 0))
    def _():
        copies = [pltpu.make_async_copy(win_hbm, win_ref, w_sem.at[0]),
                  pltpu.make_async_copy(wout_hbm, wout_ref, w_sem.at[1])]
        for cp in copies:
            cp.start()
        for cp in copies:
            cp.wait()

    @pl.when(t == 0)
    def _():
        for j in range(n_slab):
            xc_ref[j * pitch:j * pitch + HIST_PAD, :] = hist_ref[0, :, j * LANES:(j + 1) * LANES]

    sh = mod_ref[0, 3:4, :]
    sc = mod_ref[0, 4:5, :]
    gt = mod_ref[0, 5:6, :]
    h_ref[...] = (_rms(x_ref[...], gmix_ref[...]) * (1.0 + sc) + sh).astype(BF16)

    zb_ref[...] = jnp.dot(h_ref[...], win_ref[:, 2 * d_a:2 * d_a + 2 * d_b],
                          preferred_element_type=F32)
    glu = zb_ref[:, 0:d_b] * _sigmoid(zb_ref[:, d_b:2 * d_b])
    for j in range(n_slab):
        xc_ref[j * pitch + HIST_PAD:j * pitch + HIST_PAD + tm, :] = glu[:, j * LANES:(j + 1) * LANES]
        conv_ref[0, :, j * LANES:(j + 1) * LANES] = (
            xc_ref[j * pitch + tm + HIST_OFF:j * pitch + tm + HIST_PAD, :])

    za_ref[...] = jnp.dot(h_ref[...], win_ref[:, 0:2 * d_a], preferred_element_type=F32)

    bias_b = bdw_ref[...]
    out_pitch = tm + CV_PITCH_PAD
    for t0 in range(0, tm, CONV_TOKENS):
        accs = [bias_b] * CONV_TOKENS
        for k0 in range(0, CONV_WIDTH, CONV_TAP_GROUP):
            k1 = min(k0 + CONV_TAP_GROUP, CONV_WIDTH)
            rows = {r: xc_ref[pl.ds(r + HIST_OFF, n_slab, stride=pitch), :]
                    for r in range(t0 + k0, t0 + CONV_TOKENS + k1 - 1)}
            for k in range(k0, k1):
                w_k = wdw_ref[k]
                accs = [accs[i] + w_k * rows[t0 + i + k] for i in range(CONV_TOKENS)]
        for i in range(CONV_TOKENS):
            cv_ref[pl.ds(t0 + i, n_slab, stride=out_pitch), :] = accs[i]

    for j in range(n_slab):
        xc_ref[j * pitch:j * pitch + HIST_PAD, :] = xc_ref[j * pitch + tm:j * pitch + tm + HIST_PAD, :]

    row_i = lax.broadcasted_iota(jnp.int32, (chunk_len, chunk_len), 0)
    col_j = lax.broadcasted_iota(jnp.int32, (chunk_len, chunk_len), 1)
    tril = row_i >= col_j
    ws = [jnp.where(tril, ws_ref[g, 0:chunk_len, 0:chunk_len], 0.0).astype(BF16)
          for g in range(N_GROUPS_A)]
    for c in range(tm // chunk_len):
        rows = pl.ds(c * chunk_len, chunk_len)
        vn = _rms(za_ref[rows, d_a:2 * d_a], gv_ref[...])
        if emit_v:
            v_ref[0, rows, :] = vn
        vb = vn.astype(BF16)
        sp = jnp.concatenate(
            [jnp.dot(ws[g], vb[:, g * HEAD_DIM_A:(g + 1) * HEAD_DIM_A],
                     preferred_element_type=F32) for g in range(N_GROUPS_A)], axis=1)
        ya = za_ref[rows, 0:d_a] * (sp + bias_ref[0:chunk_len, :])
        ycat_ref[rows, 0:d_a] = _rms(ya, goa_ref[...]).astype(BF16)

    cv = jnp.concatenate([cv_ref[j * out_pitch:j * out_pitch + tm, :] for j in range(n_slab)],
                         axis=1)
    mu = jnp.mean(cv, axis=-1, keepdims=True)
    cc = cv - mu
    ln = cc * lax.rsqrt(jnp.mean(cc * cc, axis=-1, keepdims=True) + EPS) * gcn_ref[...] + bcn_ref[...]
    ycat_ref[:, d_a:d_a + d_b] = _rms(_silu(ln), gob_ref[...]).astype(BF16)

    y = jnp.dot(ycat_ref[...], wout_ref[...], preferred_element_type=F32)
    o_ref[...] = x_ref[...] + gt * y


def _mixing(x, mod, hist, g_mix, w_in, g_v, w_s, bias, w_dw, b_dw, g_cn, b_cn, g_out_a, g_out_b,
            w_out, *, batch, tm, chunk_len, emit_v):
    t, d = x.shape
    seq = t // batch
    nt = seq // tm
    d_a = g_v.shape[0]
    d_b = g_out_b.shape[0]
    n_slab = d_b // LANES
    const2 = lambda b, i: (0, 0)
    kern = functools.partial(_mix_kernel, chunk_len=chunk_len, emit_v=emit_v)
    out_shape = [jax.ShapeDtypeStruct((t, d), F32),
                 jax.ShapeDtypeStruct((batch, HIST, d_b), F32)]
    out_specs = [pl.BlockSpec((tm, d), lambda b, i: (b * nt + i, 0)),
                 pl.BlockSpec((1, HIST, d_b), lambda b, i: (b, 0, 0))]
    if emit_v:
        out_shape.append(jax.ShapeDtypeStruct((batch, seq, d_a), F32))
        out_specs.append(pl.BlockSpec((1, tm, d_a), lambda b, i: (b, i, 0)))
    return pl.pallas_call(
        kern,
        grid=(batch, nt),
        in_specs=[
            pl.BlockSpec((tm, d), lambda b, i: (b * nt + i, 0)),
            pl.BlockSpec((1, N_MOD, d), lambda b, i: (b, 0, 0)),
            pl.BlockSpec((1, d), const2),
            pl.BlockSpec(memory_space=pl.ANY),
            pl.BlockSpec((1, d_a), const2),
            pl.BlockSpec(w_s.shape, lambda b, i: (0, 0, 0)),
            pl.BlockSpec(bias.shape, const2),
            pl.BlockSpec((1, HIST_PAD, d_b), lambda b, i: (b, 0, 0)),
            pl.BlockSpec((w_dw.shape[0], n_slab, LANES), lambda b, i: (0, 0, 0)),
            pl.BlockSpec((n_slab, LANES), const2),
            pl.BlockSpec((1, d_b), const2),
            pl.BlockSpec((1, d_b), const2),
            pl.BlockSpec((1, d_a), const2),
            pl.BlockSpec((1, d_b), const2),
            pl.BlockSpec(memory_space=pl.ANY),
        ],
        out_specs=out_specs,
        out_shape=out_shape,
        scratch_shapes=[
            pltpu.VMEM((tm, d), BF16),
            pltpu.VMEM((tm, 2 * d_a), F32),
            pltpu.VMEM((tm, 2 * d_b), F32),
            pltpu.VMEM((tm, d_a + d_b), BF16),
            pltpu.VMEM((n_slab * (HIST_PAD + tm + XC_PITCH_PAD), LANES), F32),
            pltpu.VMEM((n_slab * (tm + CV_PITCH_PAD), LANES), F32),
            pltpu.VMEM(w_in.shape, BF16),
            pltpu.VMEM(w_out.shape, BF16),
            pltpu.SemaphoreType.DMA((2,)),
        ],
        compiler_params=pltpu.CompilerParams(
            dimension_semantics=("arbitrary", "arbitrary"), vmem_limit_bytes=VMEM_LIMIT_BYTES),
        name="mixing_sublayer",
    )(x, mod, g_mix.reshape(1, d), w_in, g_v.reshape(1, d_a), w_s, bias, hist,
      w_dw.reshape(w_dw.shape[0], n_slab, LANES), b_dw.reshape(n_slab, LANES),
      g_cn.reshape(1, d_b), b_cn.reshape(1, d_b), g_out_a.reshape(1, d_a),
      g_out_b.reshape(1, d_b), w_out)


def kernel(x_prompt, x_sample, cache_conv, c_prompt, c_sample, w_ada, b_ada, g_ffn1, w_up1, w_down1, g_mix, w_in, g_v, w_s, b_s, w_dw, b_dw, g_cn, b_cn, g_out_a, g_out_b, w_out, g_ffn2, w_up2, w_down2, g_final):
    depth = w_ada.shape[0]
    bp, seq, d = x_prompt.shape
    bs, dec_seq, _ = x_sample.shape
    d_b = w_dw.shape[-1]
    mlp_chunk = w_s.shape[-1]

    xp = x_prompt.reshape(bp * seq, d)
    xs = x_sample.reshape(bs * dec_seq, d)
    c_all = jnp.concatenate([c_prompt, c_sample], axis=0)
    hist_p = jnp.zeros((bp, HIST_PAD, d_b), F32)

    conv_p, conv_s, v_s = [], [], []
    for l in range(depth):
        mod = _modulation(c_all, w_ada[l], b_ada[l]).reshape(bp + bs, N_MOD, d)
        mod_p, mod_s = mod[:bp], mod[bp:]
        wi, wo = w_in[l].astype(BF16), w_out[l].astype(BF16)
        bias = jnp.repeat(b_s[l].T, HEAD_DIM_A, axis=1)
        hist_s = jnp.pad(cache_conv[l], ((0, 0), (HIST_OFF, 0), (0, 0)))
        last = l == depth - 1
        mix_w = (g_mix[l], wi, g_v[l], w_s[l], bias, w_dw[l], b_dw[l], g_cn[l], b_cn[l],
                 g_out_a[l], g_out_b[l], wo)
        prompt_tiles = dict(tm=FFN_TM, tf=FFN_TF, seg_rows=FFN_TM, tiles_per_mod=seq // FFN_TM)
        sample_tiles = dict(tm=bs * dec_seq, tf=FFN_TF, seg_rows=dec_seq, tiles_per_mod=1)

        xs, wu1, wd1 = _ffn(xs, mod_s, g_ffn1[l], w_up1[l], w_down1[l], g_final, mod_base=0,
                            final_norm=False, cast_weights=True, **sample_tiles)
        xp = _ffn(xp, mod_p, g_ffn1[l], wu1, wd1, g_final, mod_base=0, final_norm=False,
                  **prompt_tiles)

        xp, cp = _mixing(xp, mod_p, hist_p, *mix_w, batch=bp, tm=MIX_TM, chunk_len=mlp_chunk,
                         emit_v=False)
        xs, cs, vs = _mixing(xs, mod_s, hist_s, *mix_w, batch=bs, tm=dec_seq,
                             chunk_len=dec_seq, emit_v=True)

        xs, wu2, wd2 = _ffn(xs, mod_s, g_ffn2[l], w_up2[l], w_down2[l], g_final, mod_base=6,
                            final_norm=last, cast_weights=True, **sample_tiles)
        xp = _ffn(xp, mod_p, g_ffn2[l], wu2, wd2, g_final, mod_base=6, final_norm=last,
                  **prompt_tiles)
        conv_p.append(cp)
        conv_s.append(cs)
        v_s.append(vs)

    if depth == 0:
        raise ValueError("depth must be at least 1")
    return (xp.reshape(bp, seq, d), xs.reshape(bs, dec_seq, d), jnp.stack(conv_p),
            jnp.stack(conv_s), jnp.stack(v_s))
```

```python
import functools

import jax
import jax.numpy as jnp
from jax import lax
from jax.experimental import pallas as pl
from jax.experimental.pallas import tpu as pltpu

EPS = 1e-6
N_MOD = 9
N_GROUPS_A = 8
HEAD_DIM_A = 128
CONV_WIDTH = 31
HIST = CONV_WIDTH - 1
HIST_PAD = 32
HIST_OFF = HIST_PAD - HIST

F32 = jnp.float32
BF16 = jnp.bfloat16

VMEM_LIMIT_BYTES = 56 * 1024 * 1024

MOD_TN = 1024
FFN_TM = 1024
FFN_TF = 512
FFN_DOWN_SPLIT = 2
FFN_PROLOGUE_ROWS = 64
FFN_OPEN_PARTS = 2
MIX_TM = 512
OUT_TM = 1024
LANES = 128
XC_PITCH_PAD = 4
CV_PITCH_PAD = 8
CONV_TOKENS = 8
CONV_TAP_GROUP = 16


def _sigmoid(x):
    return 1.0 / (1.0 + jnp.exp(-x))


def _silu(x):
    return x * _sigmoid(x)


def _rms(x, g):
    return (x * lax.rsqrt(jnp.mean(x * x, axis=-1, keepdims=True) + EPS)) * g


def _mod_kernel(c_ref, w_ref, b_ref, o_ref):
    s = _silu(c_ref[...]).astype(BF16)
    o_ref[...] = jnp.dot(s, w_ref[...].astype(BF16), preferred_element_type=F32) + b_ref[...]


def _modulation(c, w_ada, b_ada):
    rows, d = c.shape
    n = w_ada.shape[1]
    return pl.pallas_call(
        _mod_kernel,
        grid=(n // MOD_TN,),
        in_specs=[
            pl.BlockSpec((rows, d), lambda j: (0, 0)),
            pl.BlockSpec((d, MOD_TN), lambda j: (0, j)),
            pl.BlockSpec((1, MOD_TN), lambda j: (0, j)),
        ],
        out_specs=pl.BlockSpec((rows, MOD_TN), lambda j: (0, j)),
        out_shape=jax.ShapeDtypeStruct((rows, n), F32),
        compiler_params=pltpu.CompilerParams(
            dimension_semantics=("arbitrary",), vmem_limit_bytes=VMEM_LIMIT_BYTES),
        name="modulation",
    )(c, w_ada, b_ada.reshape(1, n))


def _ffn_kernel(x_hbm, mod_ref, g_ref, *refs, mod_base, seg_rows, final_norm, cast_weights):
    if cast_weights:
        wg32_ref, wu32_ref, wd32_ref, gf_ref, o_ref, wgu_ref, wd_ref, x_buf, h_ref, x_sem = refs
        tf = wd_ref.shape[0]
        wgu_ref[:, 0:tf] = wg32_ref[...].astype(BF16)
        wgu_ref[:, tf:2 * tf] = wu32_ref[...].astype(BF16)
        wd_ref[...] = wd32_ref[...].astype(BF16)
    else:
        wgu_ref, wd_ref, gf_ref, o_ref, x_buf, h_ref, x_sem = refs
        tf = wd_ref.shape[0]
    i = pl.program_id(0)
    j = pl.program_id(1)
    tm, d = o_ref.shape
    n_seg = mod_ref.shape[0]

    def x_copy(tile):
        return pltpu.make_async_copy(x_hbm.at[pl.ds(tile * tm, tm), :], x_buf, x_sem)

    @pl.when(jnp.logical_and(i == 0, j == 0))
    def _():
        x_copy(0).start()

    def swiglu_rows(r0, n_rows, first):
        h = h_ref[pl.ds(r0, n_rows), :]
        gu = jnp.dot(h, wgu_ref[...], preferred_element_type=F32)
        a = (_silu(gu[:, 0:tf]) * gu[:, tf:2 * tf]).astype(BF16)
        for n in range(FFN_DOWN_SPLIT):
            cols = pl.ds(n * (d // FFN_DOWN_SPLIT), d // FFN_DOWN_SPLIT)
            y = jnp.dot(a, wd_ref[:, cols], preferred_element_type=F32)
            start = r0
            while start < r0 + n_rows:
                s = start // seg_rows
                stop = min((s + 1) * seg_rows, r0 + n_rows)
                rows = pl.ds(start, stop - start)
                half_gate = 0.5 * mod_ref[s, mod_base + 2:mod_base + 3, cols]
                update = half_gate * y[start - r0:stop - r0, :]
                if first:
                    o_ref[rows, cols] = x_buf[rows, cols] + update
                else:
                    o_ref[rows, cols] += update
                start = stop

    def open_rows(s, r0, n_rows):
        sh = mod_ref[s, mod_base:mod_base + 1, :]
        scale = g_ref[...] * (1.0 + mod_ref[s, mod_base + 1:mod_base + 2, :])
        chunk = min(n_rows, FFN_PROLOGUE_ROWS)
        for c in range(n_rows // chunk):
            rows = pl.ds(r0 + c * chunk, chunk)
            x = x_buf[rows, :]
            inv = lax.rsqrt(jnp.mean(x * x, axis=-1, keepdims=True) + EPS)
            h_ref[rows, :] = ((x * inv) * scale + sh).astype(BF16)

    @pl.when(j == 0)
    def _():
        x_copy(i).wait()
        if n_seg == 1:
            part = tm // FFN_OPEN_PARTS
            for p in range(FFN_OPEN_PARTS):
                open_rows(0, p * part, part)
                swiglu_rows(p * part, part, first=True)
        else:
            for s in range(n_seg):
                open_rows(s, s * seg_rows, seg_rows)
            swiglu_rows(0, tm, first=True)

    @pl.when(jnp.logical_and(j == 1, i + 1 < pl.num_programs(0)))
    def _():
        x_copy(i + 1).start()

    @pl.when(j > 0)
    def _():
        swiglu_rows(0, tm, first=False)

    if final_norm:
        @pl.when(j == pl.num_programs(1) - 1)
        def _():
            o_ref[...] = _rms(o_ref[...], gf_ref[...])


def _ffn(x, mod, g, w_up, w_down, g_final, *, tm, tf, seg_rows, tiles_per_mod, mod_base,
         final_norm, cast_weights=False):
    t, d = x.shape
    f = w_down.shape[0]
    nf = f // tf
    assert nf >= 2 and f % tf == 0 and t % tm == 0 and tm % seg_rows == 0
    assert not cast_weights or t == tm
    n_seg = tm // seg_rows
    kern = functools.partial(_ffn_kernel, mod_base=mod_base, seg_rows=seg_rows,
                             final_norm=final_norm, cast_weights=cast_weights)
    if cast_weights:
        weight_specs = [pl.BlockSpec((d, tf), lambda i, j: (0, j)),
                        pl.BlockSpec((d, tf), lambda i, j: (0, nf + j)),
                        pl.BlockSpec((tf, d), lambda i, j: (j, 0))]
        weights = (w_up, w_up, w_down)
    else:
        weight_specs = [pl.BlockSpec((d, 2 * tf), lambda i, j: (0, j)),
                        pl.BlockSpec((tf, d), lambda i, j: (j, 0))]
        weights = (w_up, w_down)
    out_specs = [pl.BlockSpec((tm, d), lambda i, j: (i, 0))]
    out_shape = [jax.ShapeDtypeStruct((t, d), F32)]
    if cast_weights:
        out_specs += [pl.BlockSpec((d, 2 * tf), lambda i, j: (0, j)),
                      pl.BlockSpec((tf, d), lambda i, j: (j, 0))]
        out_shape += [jax.ShapeDtypeStruct((d, 2 * f), BF16), jax.ShapeDtypeStruct((f, d), BF16)]
    out = pl.pallas_call(
        kern,
        grid=(t // tm, nf),
        in_specs=[
            pl.BlockSpec(memory_space=pl.ANY),
            pl.BlockSpec((n_seg, N_MOD, d), lambda i, j: (i // tiles_per_mod, 0, 0)),
            pl.BlockSpec((1, d), lambda i, j: (0, 0)),
            *weight_specs,
            pl.BlockSpec((1, d), lambda i, j: (0, 0)),
        ],
        out_specs=out_specs,
        out_shape=out_shape,
        scratch_shapes=[pltpu.VMEM((tm, d), F32), pltpu.VMEM((tm, d), BF16),
                        pltpu.SemaphoreType.DMA(())],
        compiler_params=pltpu.CompilerParams(
            dimension_semantics=("arbitrary", "arbitrary"), vmem_limit_bytes=VMEM_LIMIT_BYTES),
        name="swiglu_half_step",
    )(x, mod, g.reshape(1, d), *weights, g_final.reshape(1, d))
    return out if cast_weights else out[0]


def _mix_kernel(x_ref, mod_ref, gmix_ref, win_ref, gv_ref, ws_ref, bias_ref, hist_ref, wdw_ref,
                bdw_ref, gcn_ref, bcn_ref, goa_ref, gob_ref,
                ycat_ref, conv_ref, *rest, chunk_len, emit_v):
    if emit_v:
        v_ref, h_ref, za_ref, zb_ref, xc_ref, cv_ref = rest
    else:
        v_ref = None
        h_ref, za_ref, zb_ref, xc_ref, cv_ref = rest
    tm = x_ref.shape[0]
    d_a = gv_ref.shape[1]
    d_b = gob_ref.shape[1]
    n_slab = d_b // LANES
    pitch = HIST_PAD + tm + XC_PITCH_PAD
    t = pl.program_id(1)

    @pl.when(t == 0)
    def _():
        for j in range(n_slab):
            xc_ref[j * pitch:j * pitch + HIST_PAD, :] = hist_ref[0, :, j * LANES:(j + 1) * LANES]

    sh = mod_ref[0, 3:4, :]
    sc = mod_ref[0, 4:5, :]
    h_ref[...] = (_rms(x_ref[...], gmix_ref[...]) * (1.0 + sc) + sh).astype(BF16)

    zb_ref[...] = jnp.dot(h_ref[...], win_ref[:, 2 * d_a:2 * d_a + 2 * d_b],
                          preferred_element_type=F32)
    glu = zb_ref[:, 0:d_b] * _sigmoid(zb_ref[:, d_b:2 * d_b])
    for j in range(n_slab):
        xc_ref[j * pitch + HIST_PAD:j * pitch + HIST_PAD + tm, :] = glu[:, j * LANES:(j + 1) * LANES]
        conv_ref[0, :, j * LANES:(j + 1) * LANES] = (
            xc_ref[j * pitch + tm + HIST_OFF:j * pitch + tm + HIST_PAD, :])

    za_ref[...] = jnp.dot(h_ref[...], win_ref[:, 0:2 * d_a], preferred_element_type=F32)

    bias_b = bdw_ref[...]
    out_pitch = tm + CV_PITCH_PAD
    for t0 in range(0, tm, CONV_TOKENS):
        accs = [bias_b] * CONV_TOKENS
        for k0 in range(0, CONV_WIDTH, CONV_TAP_GROUP):
            k1 = min(k0 + CONV_TAP_GROUP, CONV_WIDTH)
            rows = {r: xc_ref[pl.ds(r + HIST_OFF, n_slab, stride=pitch), :]
                    for r in range(t0 + k0, t0 + CONV_TOKENS + k1 - 1)}
            for k in range(k0, k1):
                w_k = wdw_ref[k]
                accs = [accs[i] + w_k * rows[t0 + i + k] for i in range(CONV_TOKENS)]
        for i in range(CONV_TOKENS):
            cv_ref[pl.ds(t0 + i, n_slab, stride=out_pitch), :] = accs[i]

    for j in range(n_slab):
        xc_ref[j * pitch:j * pitch + HIST_PAD, :] = xc_ref[j * pitch + tm:j * pitch + tm + HIST_PAD, :]

    row_i = lax.broadcasted_iota(jnp.int32, (chunk_len, chunk_len), 0)
    col_j = lax.broadcasted_iota(jnp.int32, (chunk_len, chunk_len), 1)
    tril = row_i >= col_j
    ws = [jnp.where(tril, ws_ref[g, 0:chunk_len, 0:chunk_len], 0.0).astype(BF16)
          for g in range(N_GROUPS_A)]
    for c in range(tm // chunk_len):
        rows = pl.ds(c * chunk_len, chunk_len)
        vn = _rms(za_ref[rows, d_a:2 * d_a], gv_ref[...])
        if emit_v:
            v_ref[0, rows, :] = vn
        vb = vn.astype(BF16)
        sp = jnp.concatenate(
            [jnp.dot(ws[g], vb[:, g * HEAD_DIM_A:(g + 1) * HEAD_DIM_A],
                     preferred_element_type=F32) for g in range(N_GROUPS_A)], axis=1)
        ya = za_ref[rows, 0:d_a] * (sp + bias_ref[0:chunk_len, :])
        ycat_ref[rows, 0:d_a] = _rms(ya, goa_ref[...]).astype(BF16)

    cv = jnp.concatenate([cv_ref[j * out_pitch:j * out_pitch + tm, :] for j in range(n_slab)],
                         axis=1)
    mu = jnp.mean(cv, axis=-1, keepdims=True)
    cc = cv - mu
    ln = cc * lax.rsqrt(jnp.mean(cc * cc, axis=-1, keepdims=True) + EPS) * gcn_ref[...] + bcn_ref[...]
    ycat_ref[:, d_a:d_a + d_b] = _rms(_silu(ln), gob_ref[...]).astype(BF16)


def _mixing(x, mod, hist, g_mix, w_in, g_v, w_s, bias, w_dw, b_dw, g_cn, b_cn, g_out_a, g_out_b,
            w_out, *, batch, tm, chunk_len, emit_v):
    t, d = x.shape
    seq = t // batch
    nt = seq // tm
    d_a = g_v.shape[0]
    d_b = g_out_b.shape[0]
    n_slab = d_b // LANES
    const2 = lambda b, i: (0, 0)
    single = pl.Buffered(1)
    kern = functools.partial(_mix_kernel, chunk_len=chunk_len, emit_v=emit_v)
    out_shape = [jax.ShapeDtypeStruct((t, d_a + d_b), BF16),
                 jax.ShapeDtypeStruct((batch, HIST, d_b), F32)]
    out_specs = [pl.BlockSpec((tm, d_a + d_b), lambda b, i: (b * nt + i, 0)),
                 pl.BlockSpec((1, HIST, d_b), lambda b, i: (b, 0, 0))]
    if emit_v:
        out_shape.append(jax.ShapeDtypeStruct((batch, seq, d_a), F32))
        out_specs.append(pl.BlockSpec((1, tm, d_a), lambda b, i: (b, i, 0)))
    outs = pl.pallas_call(
        kern,
        grid=(batch, nt),
        in_specs=[
            pl.BlockSpec((tm, d), lambda b, i: (b * nt + i, 0)),
            pl.BlockSpec((1, N_MOD, d), lambda b, i: (b, 0, 0)),
            pl.BlockSpec((1, d), const2),
            pl.BlockSpec(w_in.shape, const2, pipeline_mode=single),
            pl.BlockSpec((1, d_a), const2),
            pl.BlockSpec(w_s.shape, lambda b, i: (0, 0, 0)),
            pl.BlockSpec(bias.shape, const2),
            pl.BlockSpec((1, HIST_PAD, d_b), lambda b, i: (b, 0, 0)),
            pl.BlockSpec((w_dw.shape[0], n_slab, LANES), lambda b, i: (0, 0, 0)),
            pl.BlockSpec((n_slab, LANES), const2),
            pl.BlockSpec((1, d_b), const2),
            pl.BlockSpec((1, d_b), const2),
            pl.BlockSpec((1, d_a), const2),
            pl.BlockSpec((1, d_b), const2),
        ],
        out_specs=out_specs,
        out_shape=out_shape,
        scratch_shapes=[
            pltpu.VMEM((tm, d), BF16),
            pltpu.VMEM((tm, 2 * d_a), F32),
            pltpu.VMEM((tm, 2 * d_b), F32),
            pltpu.VMEM((n_slab * (HIST_PAD + tm + XC_PITCH_PAD), LANES), F32),
            pltpu.VMEM((n_slab * (tm + CV_PITCH_PAD), LANES), F32),
        ],
        compiler_params=pltpu.CompilerParams(
            dimension_semantics=("arbitrary", "arbitrary"), vmem_limit_bytes=VMEM_LIMIT_BYTES),
        name="mixing_sublayer",
    )(x, mod, g_mix.reshape(1, d), w_in, g_v.reshape(1, d_a), w_s, bias, hist,
      w_dw.reshape(w_dw.shape[0], n_slab, LANES), b_dw.reshape(n_slab, LANES),
      g_cn.reshape(1, d_b), b_cn.reshape(1, d_b), g_out_a.reshape(1, d_a),
      g_out_b.reshape(1, d_b))
    x_out = _out_proj(x, outs[0], mod, w_out, tm=min(seq, OUT_TM), rows_per_mod=seq)
    return (x_out, *outs[1:])


def _out_proj_kernel(x_ref, y_ref, mod_ref, w_ref, o_ref):
    y = jnp.dot(y_ref[...], w_ref[...], preferred_element_type=F32)
    o_ref[...] = x_ref[...] + mod_ref[0, 5:6, :] * y


def _out_proj(x, ycat, mod, w_out, *, tm, rows_per_mod):
    t, d = x.shape
    k = ycat.shape[1]
    tiles_per_mod = rows_per_mod // tm
    return pl.pallas_call(
        _out_proj_kernel,
        grid=(t // tm,),
        in_specs=[
            pl.BlockSpec((tm, d), lambda i: (i, 0)),
            pl.BlockSpec((tm, k), lambda i: (i, 0)),
            pl.BlockSpec((1, N_MOD, d), lambda i: (i // tiles_per_mod, 0, 0)),
            pl.BlockSpec((k, d), lambda i: (0, 0), pipeline_mode=pl.Buffered(1)),
        ],
        out_specs=pl.BlockSpec((tm, d), lambda i: (i, 0)),
        out_shape=jax.ShapeDtypeStruct((t, d), F32),
        compiler_params=pltpu.CompilerParams(
            dimension_semantics=("arbitrary",), vmem_limit_bytes=VMEM_LIMIT_BYTES),
        name="output_projection",
    )(x, ycat, mod, w_out)


def kernel(x_prompt, x_sample, cache_conv, c_prompt, c_sample, w_ada, b_ada, g_ffn1, w_up1, w_down1, g_mix, w_in, g_v, w_s, b_s, w_dw, b_dw, g_cn, b_cn, g_out_a, g_out_b, w_out, g_ffn2, w_up2, w_down2, g_final):
    depth = w_ada.shape[0]
    bp, seq, d = x_prompt.shape
    bs, dec_seq, _ = x_sample.shape
    d_b = w_dw.shape[-1]
    mlp_chunk = w_s.shape[-1]

    xp = x_prompt.reshape(bp * seq, d)
    xs = x_sample.reshape(bs * dec_seq, d)
    c_all = jnp.concatenate([c_prompt, c_sample], axis=0)
    hist_p = jnp.zeros((bp, HIST_PAD, d_b), F32)

    conv_p, conv_s, v_s = [], [], []
    for l in range(depth):
        mod = _modulation(c_all, w_ada[l], b_ada[l]).reshape(bp + bs, N_MOD, d)
        mod_p, mod_s = mod[:bp], mod[bp:]
        wi, wo = w_in[l].astype(BF16), w_out[l].astype(BF16)
        bias = jnp.repeat(b_s[l].T, HEAD_DIM_A, axis=1)
        hist_s = jnp.pad(cache_conv[l], ((0, 0), (HIST_OFF, 0), (0, 0)))
        last = l == depth - 1
        mix_w = (g_mix[l], wi, g_v[l], w_s[l], bias, w_dw[l], b_dw[l], g_cn[l], b_cn[l],
                 g_out_a[l], g_out_b[l], wo)
        prompt_tiles = dict(tm=FFN_TM, tf=FFN_TF, seg_rows=FFN_TM, tiles_per_mod=seq // FFN_TM)
        sample_tiles = dict(tm=bs * dec_seq, tf=FFN_TF, seg_rows=dec_seq, tiles_per_mod=1)

        xs, wu1, wd1 = _ffn(xs, mod_s, g_ffn1[l], w_up1[l], w_down1[l], g_final, mod_base=0,
                            final_norm=False, cast_weights=True, **sample_tiles)
        xp = _ffn(xp, mod_p, g_ffn1[l], wu1, wd1, g_final, mod_base=0, final_norm=False,
                  **prompt_tiles)

        xp, cp = _mixing(xp, mod_p, hist_p, *mix_w, batch=bp, tm=MIX_TM, chunk_len=mlp_chunk,
                         emit_v=False)
        xs, cs, vs = _mixing(xs, mod_s, hist_s, *mix_w, batch=bs, tm=dec_seq,
                             chunk_len=dec_seq, emit_v=True)

        xs, wu2, wd2 = _ffn(xs, mod_s, g_ffn2[l], w_up2[l], w_down2[l], g_final, mod_base=6,
                            final_norm=last, cast_weights=True, **sample_tiles)
        xp = _ffn(xp, mod_p, g_ffn2[l], wu2, wd2, g_final, mod_base=6, final_norm=last,
                  **prompt_tiles)
        conv_p.append(cp)
        conv_s.append(cs)
        v_s.append(vs)

    if depth == 0:
        raise ValueError("depth must be at least 1")
    return (xp.reshape(bp, seq, d), xs.reshape(bs, dec_seq, d), jnp.stack(conv_p),
            jnp.stack(conv_s), jnp.stack(v_s))
```
